```python
import math
import jax, jax.numpy as jnp
from jax import lax
import numpy as np

D_MODEL = 1024
BATCH = 2
SEQ = 8192
DEPTH = 4

D_FF = 2816
MLA_HEADS = 4
MLA_Q_LORA = 384
MLA_KV_LORA = 256
MLA_NOPE = 128
MLA_ROPE = 64
MLA_V = 128
MLA_W = MLA_HEADS * MLA_V
MLSTM_HEADS = 4
MLSTM_DH = 128
MLSTM_W = MLSTM_HEADS * MLSTM_DH
MLSTM_CONV = 4
RET_HEADS = 4
RET_DH = 128
RET_W = RET_HEADS * RET_DH
N_BRANCH = 3
Q_BLOCK = 128
CHUNK = 128
ROPE_THETA = 10000.0
NORM_EPS = 1e-5
NEG_INF = -1e30
DN_ALPHA = (2 * DEPTH) ** 0.25
DN_BETA = (8 * DEPTH) ** -0.25
IN_SPLITS = (MLA_Q_LORA, MLA_KV_LORA, MLA_ROPE,
             MLSTM_W, MLSTM_W, MLSTM_W,
             MLSTM_HEADS, MLSTM_HEADS, MLSTM_W,
             RET_W, RET_W, RET_W, RET_W,
             N_BRANCH * D_MODEL)
IN_COLS = sum(IN_SPLITS)

kernel_name = 'hybrid_mla_mlstm_retention_macaron_deepnorm'


def _split_cols(z, sizes):
    out = []
    start = 0
    for n in sizes:
        out.append(z[..., start:start + n])
        start += n
    return out


def _layer_norm(x, g, b):
    xf = x.astype(jnp.float32)
    mu = jnp.mean(xf, axis=-1, keepdims=True)
    var = jnp.mean(jnp.square(xf - mu), axis=-1, keepdims=True)
    return ((xf - mu) * lax.rsqrt(var + NORM_EPS) * g + b).astype(x.dtype)


def _rms_norm(x, g):
    xf = x.astype(jnp.float32)
    y = xf * lax.rsqrt(jnp.mean(jnp.square(xf), axis=-1, keepdims=True) + NORM_EPS)
    return (y * g).astype(x.dtype)


def _head_norm(x, g, n_heads):
    b, s, w = x.shape
    xf = x.astype(jnp.float32).reshape(b, s, n_heads, w // n_heads)
    mu = jnp.mean(xf, axis=-1, keepdims=True)
    var = jnp.mean(jnp.square(xf - mu), axis=-1, keepdims=True)
    y = ((xf - mu) * lax.rsqrt(var + NORM_EPS)).reshape(b, s, w)
    return (y * g).astype(x.dtype)


def _swiglu(x, w_gate, w_up, w_down):
    return (jax.nn.silu(x @ w_gate) * (x @ w_up)) @ w_down


def _rope(x, pos):
    half = x.shape[-1] // 2
    inv_freq = ROPE_THETA ** (-jnp.arange(half, dtype=jnp.float32) / half)
    ang = pos.astype(jnp.float32)[:, :, None, None] * inv_freq
    cos, sin = jnp.cos(ang), jnp.sin(ang)
    xf = x.astype(jnp.float32)
    x1, x2 = xf[..., :half], xf[..., half:]
    return jnp.concatenate([x1 * cos - x2 * sin, x2 * cos + x1 * sin], axis=-1).astype(x.dtype)


def _causal_dwconv(x, w, b):
    k, c = w.shape
    y = lax.conv_general_dilated(x, w[:, None, :], window_strides=(1,), padding=[(k - 1, 0)],
                                 dimension_numbers=('NWC', 'WIO', 'NWC'), feature_group_count=c)
    return y + b


def _to_chunks(t):
    b, s = t.shape[:2]
    t = t.reshape(b, s // CHUNK, CHUNK, *t.shape[2:])
    return jnp.moveaxis(t, 3, 1)


def _from_chunks(t):
    b, h, nc, l, d = t.shape
    return jnp.moveaxis(t, 1, 3).reshape(b, nc * l, h * d)


def _mla_attention(q_nope, q_rope, k_nope, k_rope, v):
    b, s, h, _ = q_nope.shape
    nb = s // Q_BLOCK
    scale = (MLA_NOPE + MLA_ROPE) ** -0.5
    key_pos = jnp.arange(s)

    def blocks(t):
        return jnp.moveaxis(t.reshape(b, nb, Q_BLOCK, *t.shape[2:]), 1, 0)

    def attend(args):
        qn, qr, blk = args
        sc = (jnp.einsum('bqhd,bkhd->bhqk', qn, k_nope)
              + jnp.einsum('bqhd,bkd->bhqk', qr, k_rope)).astype(jnp.float32) * scale
        q_pos = blk * Q_BLOCK + jnp.arange(Q_BLOCK)
        sc = jnp.where(key_pos[None, :] <= q_pos[:, None], sc, NEG_INF)
        p = jax.nn.softmax(sc, axis=-1).astype(v.dtype)
        return jnp.einsum('bhqk,bkhd->bqhd', p, v)

    out = lax.map(attend, (blocks(q_nope), blocks(q_rope), jnp.arange(nb)))
    return jnp.moveaxis(out, 0, 1).reshape(b, s, h * v.shape[-1])


def _mlstm(q, k, v, i_pre, f_pre):
    d = q.shape[-1]
    q, k, v = _to_chunks(q), _to_chunks(k * (d ** -0.5)), _to_chunks(v)
    ig = _to_chunks(i_pre.astype(jnp.float32))
    lf = _to_chunks(jax.nn.log_sigmoid(f_pre.astype(jnp.float32)))
    g = jnp.cumsum(lf, axis=-1)
    g_last = g[..., -1]
    w_loc = g_last[..., None] - g + ig
    m_loc = jnp.max(w_loc, axis=-1)
    e_loc = jnp.exp(w_loc - m_loc[..., None])
    c_loc = jnp.einsum('bhcl,bhcld,bhcle->bhcde', e_loc, k, v)
    n_loc = jnp.einsum('bhcl,bhcld->bhcd', e_loc, k)

    def step(carry, xs):
        c, n, m = carry
        gl, ml, cl, nl = xs
        m_new = jnp.maximum(gl + m, ml)
        a = jnp.exp(gl + m - m_new)
        bsc = jnp.exp(ml - m_new)
        c_new = a[..., None, None] * c + bsc[..., None, None] * cl
        n_new = a[..., None] * n + bsc[..., None] * nl
        return (c_new, n_new, m_new), (c, n, m)

    bb, hh = q.shape[:2]
    init = (jnp.zeros((bb, hh, d, v.shape[-1]), jnp.float32),
            jnp.zeros((bb, hh, d), jnp.float32),
            jnp.zeros((bb, hh), jnp.float32))
    xs = (jnp.moveaxis(g_last, 2, 0), jnp.moveaxis(m_loc, 2, 0),
          jnp.moveaxis(c_loc, 2, 0), jnp.moveaxis(n_loc, 2, 0))
    _, (c_prev, n_prev, m_prev) = lax.scan(step, init, xs)
    c_prev = jnp.moveaxis(c_prev, 0, 2)
    n_prev = jnp.moveaxis(n_prev, 0, 2)
    m_prev = jnp.moveaxis(m_prev, 0, 2)

    causal = jnp.tril(jnp.ones((CHUNK, CHUNK), dtype=bool))
    log_d = jnp.where(causal, g[..., :, None] - g[..., None, :] + ig[..., None, :], -jnp.inf)
    a_inter = g + m_prev[..., None]
    m_j = jnp.maximum(a_inter, jnp.max(log_d, axis=-1))
    p = jnp.exp(log_d - m_j[..., None])
    e_inter = jnp.exp(a_inter - m_j)
    qk = jnp.einsum('bhcjd,bhcsd->bhcjs', q, k) * p
    num = (jnp.einsum('bhcjs,bhcse->bhcje', qk, v)
           + e_inter[..., None] * jnp.einsum('bhcjd,bhcde->bhcje', q, c_prev))
    den = jnp.sum(qk, axis=-1) + e_inter * jnp.einsum('bhcjd,bhcd->bhcj', q, n_prev)
    h = num / jnp.maximum(jnp.abs(den), jnp.exp(-m_j))[..., None]
    return _from_chunks(h)


def _retention(q, k, v):
    d = q.shape[-1]
    n_heads = q.shape[2]
    log_gamma = jnp.log(1.0 - 2.0 ** (-5.0 - jnp.arange(n_heads, dtype=jnp.float32)))
    q, k, v = _to_chunks(q), _to_chunks(k * (d ** -0.5)), _to_chunks(v)
    idx = jnp.arange(CHUNK, dtype=jnp.float32)
    causal = jnp.tril(jnp.ones((CHUNK, CHUNK), dtype=bool))
    decay = jnp.exp(jnp.where(causal, (idx[:, None] - idx[None, :]) * log_gamma[:, None, None], -jnp.inf))
    intra = jnp.einsum('bhcjs,bhcse->bhcje',
                       jnp.einsum('bhcjd,bhcsd->bhcjs', q, k) * decay[None, :, None], v)
    w_state = jnp.exp((CHUNK - 1 - idx)[None, :] * log_gamma[:, None])
    s_loc = jnp.einsum('hl,bhcld,bhcle->bhcde', w_state, k, v)
    g_chunk = jnp.exp(CHUNK * log_gamma)[None, :, None, None]

    def step(r, s_c):
        return g_chunk * r + s_c, r

    r0 = jnp.zeros((q.shape[0], n_heads, d, v.shape[-1]), jnp.float32)
    _, r_prev = lax.scan(step, r0, jnp.moveaxis(s_loc, 2, 0))
    r_prev = jnp.moveaxis(r_prev, 0, 2)
    xi = jnp.exp((idx + 1.0)[None, :] * log_gamma[:, None])
    inter = xi[None, :, None, :, None] * jnp.einsum('bhcjd,bhcde->bhcje', q, r_prev)
    return _from_chunks(intra + inter)


def _hybrid_mixer(x, pos, w_in, mla_q_norm_g, mla_w_uq, mla_kv_norm_g, mla_w_ukv,
                  mlstm_conv_w, mlstm_conv_b, mlstm_gate_b, mlstm_norm_g, ret_norm_g,
                  w_br_mla, w_br_mlstm, w_br_ret, gate_b, w_out):
    b, s, _ = x.shape
    z = x @ w_in
    (c_q, c_kv, k_r, mq, mk, mv, mi, mf, mo, rq, rk, rv, rg, gl) = _split_cols(z, IN_SPLITS)

    qh = (_rms_norm(c_q, mla_q_norm_g) @ mla_w_uq).reshape(b, s, MLA_HEADS, MLA_NOPE + MLA_ROPE)
    q_nope, q_rope = qh[..., :MLA_NOPE], _rope(qh[..., MLA_NOPE:], pos)
    kv = (_rms_norm(c_kv, mla_kv_norm_g) @ mla_w_ukv).reshape(b, s, MLA_HEADS, MLA_NOPE + MLA_V)
    k_nope, v_mla = kv[..., :MLA_NOPE], kv[..., MLA_NOPE:]
    k_rope = _rope(k_r[:, :, None, :], pos)[:, :, 0, :]
    y_mla = _mla_attention(q_nope, q_rope, k_nope, k_rope, v_mla)

    qk_c = jax.nn.silu(_causal_dwconv(jnp.concatenate([mq, mk], axis=-1), mlstm_conv_w, mlstm_conv_b))
    mq_c, mk_c = qk_c[..., :MLSTM_W], qk_c[..., MLSTM_W:]
    gates = jnp.concatenate([mi, mf], axis=-1) + mlstm_gate_b
    i_pre, f_pre = gates[..., :MLSTM_HEADS], gates[..., MLSTM_HEADS:]
    hs = (b, s, MLSTM_HEADS, MLSTM_DH)
    h_ml = _mlstm(mq_c.reshape(hs), mk_c.reshape(hs), mv.reshape(hs), i_pre, f_pre)
    y_mlstm = jax.nn.sigmoid(mo) * _head_norm(h_ml, mlstm_norm_g, MLSTM_HEADS)

    rs = (b, s, RET_HEADS, RET_DH)
    h_rt = _retention(_rope(rq.reshape(rs), pos), _rope(rk.reshape(rs), pos), rv.reshape(rs))
    y_ret = jax.nn.silu(rg) * _head_norm(h_rt, ret_norm_g, RET_HEADS)

    gate = jax.nn.sigmoid(gl.reshape(b, s, N_BRANCH, D_MODEL) + gate_b)
    merged = (gate[:, :, 0] * (y_mla @ w_br_mla)
              + gate[:, :, 1] * (y_mlstm @ w_br_mlstm)
              + gate[:, :, 2] * (y_ret @ w_br_ret))
    return merged @ w_out


def setup_inputs(seed: int = 0) -> dict:
    key = jax.random.key(seed)
    keys = jax.random.split(key, 32)

    def nrm(i, shape, scale):
        return scale * jax.random.normal(keys[i], shape, dtype=jnp.float32)

    def gain(i, n):
        return 1.0 + nrm(i, (DEPTH, n), 0.02)

    offs = jax.random.randint(keys[1], (BATCH, 1), 0, 4096, dtype=jnp.int32)
    positions = (offs + jnp.arange(SEQ, dtype=jnp.int32)[None, :]).astype(jnp.int32)
    f_bias = jnp.linspace(3.0, 6.0, MLSTM_HEADS, dtype=jnp.float32)[None, :] + nrm(15, (DEPTH, MLSTM_HEADS), 0.1)
    mlstm_gate_b = jnp.concatenate([nrm(14, (DEPTH, MLSTM_HEADS), 0.1), f_bias], axis=-1)
    return {
        'x': nrm(0, (BATCH, SEQ, D_MODEL), 1.0),
        'positions': positions,
        'ffn1_w_gate': nrm(2, (DEPTH, D_MODEL, D_FF), D_MODEL ** -0.5),
        'ffn1_w_up': nrm(3, (DEPTH, D_MODEL, D_FF), D_MODEL ** -0.5),
        'ffn1_w_down': nrm(4, (DEPTH, D_FF, D_MODEL), DN_BETA * D_FF ** -0.5),
        'ln1_g': gain(5, D_MODEL),
        'ln1_b': nrm(6, (DEPTH, D_MODEL), 0.02),
        'w_in': nrm(7, (DEPTH, D_MODEL, IN_COLS), D_MODEL ** -0.5),
        'mla_q_norm_g': gain(8, MLA_Q_LORA),
        'mla_w_uq': nrm(9, (DEPTH, MLA_Q_LORA, MLA_HEADS * (MLA_NOPE + MLA_ROPE)), MLA_Q_LORA ** -0.5),
        'mla_kv_norm_g': gain(10, MLA_KV_LORA),
        'mla_w_ukv': nrm(11, (DEPTH, MLA_KV_LORA, MLA_HEADS * (MLA_NOPE + MLA_V)), MLA_KV_LORA ** -0.5),
        'mlstm_conv_w': nrm(12, (DEPTH, MLSTM_CONV, 2 * MLSTM_W), MLSTM_CONV ** -0.5),
        'mlstm_conv_b': nrm(13, (DEPTH, 2 * MLSTM_W), 0.02),
        'mlstm_gate_b': mlstm_gate_b,
        'mlstm_norm_g': gain(16, MLSTM_W),
        'ret_norm_g': gain(17, RET_W),
        'w_br_mla': nrm(18, (DEPTH, MLA_W, D_MODEL), MLA_W ** -0.5),
        'w_br_mlstm': nrm(19, (DEPTH, MLSTM_W, D_MODEL), MLSTM_W ** -0.5),
        'w_br_ret': nrm(20, (DEPTH, RET_W, D_MODEL), RET_W ** -0.5),
        'gate_b': nrm(21, (DEPTH, N_BRANCH, D_MODEL), 0.02),
        'w_out': nrm(22, (DEPTH, D_MODEL, D_MODEL), DN_BETA * D_MODEL ** -0.5),
        'ln2_g': gain(23, D_MODEL),
        'ln2_b': nrm(24, (DEPTH, D_MODEL), 0.02),
        'ffn2_w_gate': nrm(25, (DEPTH, D_MODEL, D_FF), D_MODEL ** -0.5),
        'ffn2_w_up': nrm(26, (DEPTH, D_MODEL, D_FF), D_MODEL ** -0.5),
        'ffn2_w_down': nrm(27, (DEPTH, D_FF, D_MODEL), DN_BETA * D_FF ** -0.5),
        'ln3_g': gain(28, D_MODEL),
        'ln3_b': nrm(29, (DEPTH, D_MODEL), 0.02),
    }


def reference(x, positions, ffn1_w_gate, ffn1_w_up, ffn1_w_down, ln1_g, ln1_b, w_in,
              mla_q_norm_g, mla_w_uq, mla_kv_norm_g, mla_w_ukv, mlstm_conv_w, mlstm_conv_b,
              mlstm_gate_b, mlstm_norm_g, ret_norm_g, w_br_mla, w_br_mlstm, w_br_ret, gate_b,
              w_out, ln2_g, ln2_b, ffn2_w_gate, ffn2_w_up, ffn2_w_down, ln3_g, ln3_b):
    for l in range(DEPTH):
        x = _layer_norm(DN_ALPHA * x + 0.5 * _swiglu(x, ffn1_w_gate[l], ffn1_w_up[l], ffn1_w_down[l]),
                        ln1_g[l], ln1_b[l])
        mix = _hybrid_mixer(x, positions, w_in[l], mla_q_norm_g[l], mla_w_uq[l], mla_kv_norm_g[l],
                            mla_w_ukv[l], mlstm_conv_w[l], mlstm_conv_b[l], mlstm_gate_b[l],
                            mlstm_norm_g[l], ret_norm_g[l], w_br_mla[l], w_br_mlstm[l], w_br_ret[l],
                            gate_b[l], w_out[l])
        x = _layer_norm(DN_ALPHA * x + mix, ln2_g[l], ln2_b[l])
        x = _layer_norm(DN_ALPHA * x + 0.5 * _swiglu(x, ffn2_w_gate[l], ffn2_w_up[l], ffn2_w_down[l]),
                        ln3_g[l], ln3_b[l])
    return x
```

```python
import functools
import math

import jax
import jax.numpy as jnp
from jax import lax
from jax.experimental import pallas as pl
from jax.experimental.pallas import tpu as pltpu

D_MODEL = 1024
D_FF = 2816
N_HEADS = 4
HEAD_DIM = 128
MIX_W = N_HEADS * HEAD_DIM
MLA_Q_LORA = 384
MLA_KV_LORA = 256
MLA_NOPE = 128
MLA_ROPE = 64
MLA_QK = 2 * HEAD_DIM
CONV_K = 4
N_BRANCH = 3
CHUNK = 128
ROPE_THETA = 10000.0
NORM_EPS = 1e-5
NEG_BIG = -1e30
DEPTH_ALPHA_POW = 0.25

LANES = 128
SUBLANES = 8
VMEM_LIMIT_BYTES = 56 * 1024 * 1024

_C_CQ = 0
_C_CKV = _C_CQ + MLA_Q_LORA
_C_KR = _C_CKV + MLA_KV_LORA
_C_MQK = _C_KR + LANES
_C_MV = _C_MQK + 2 * MIX_W
_C_MO = _C_MV + MIX_W
_C_RQ = _C_MO + MIX_W
_C_RK = _C_RQ + MIX_W
_C_RV = _C_RK + MIX_W
_C_RG = _C_RV + MIX_W
_C_GL = _C_RG + MIX_W
_C_END = _C_GL + N_BRANCH * D_MODEL

BF16 = jnp.bfloat16
F32 = jnp.float32


def _resident(shape):
    nd = len(shape)
    return pl.BlockSpec(shape, lambda *_: (0,) * nd, pipeline_mode=pl.Buffered(1))


def _params(sem):
    return pltpu.CompilerParams(dimension_semantics=sem, vmem_limit_bytes=VMEM_LIMIT_BYTES)


def _layer_norm(y, g, b):
    mu = jnp.mean(y, axis=-1, keepdims=True)
    yc = y - mu
    var = jnp.mean(yc * yc, axis=-1, keepdims=True)
    return yc * lax.rsqrt(var + NORM_EPS) * g + b


def _rms_norm(y, g):
    return y * lax.rsqrt(jnp.mean(y * y, axis=-1, keepdims=True) + NORM_EPS) * g


def _sigmoid(x):
    return 1.0 / (1.0 + jnp.exp(-x))


def _dot(a, b):
    return jnp.dot(a, b, preferred_element_type=F32)


def _dot_nt(a, b):
    return lax.dot_general(a, b, (((1,), (1,)), ((), ())), preferred_element_type=F32)


def _ffn_ln_kernel(x_ref, wg_ref, wu_ref, wd_ref, g_ref, b_ref, o_ref, *, alpha, ff_chunk):
    x = x_ref[...]
    xb = x.astype(BF16)
    acc = None
    for c in range(0, D_FF, ff_chunk):
        gate = _dot(xb, wg_ref[:, c:c + ff_chunk])
        up = _dot(xb, wu_ref[:, c:c + ff_chunk])
        act = (gate * _sigmoid(gate) * up).astype(BF16)
        part = _dot(act, wd_ref[c:c + ff_chunk, :])
        acc = part if acc is None else acc + part
    o_ref[...] = _layer_norm(alpha * x + 0.5 * acc, g_ref[...], b_ref[...])


def _ffn_ln(x, wg, wu, wd, g, b, *, alpha, tm, ff_chunk):
    n = x.shape[0]
    tm = min(tm, n)
    row = pl.BlockSpec((tm, D_MODEL), lambda i: (i, 0))
    return pl.pallas_call(
        functools.partial(_ffn_ln_kernel, alpha=alpha, ff_chunk=ff_chunk),
        grid=(n // tm,),
        in_specs=[row, _resident((D_MODEL, D_FF)), _resident((D_MODEL, D_FF)),
                  _resident((D_FF, D_MODEL)), _resident((1, D_MODEL)), _resident((1, D_MODEL))],
        out_specs=row,
        out_shape=jax.ShapeDtypeStruct((n, D_MODEL), F32),
        compiler_params=_params(("arbitrary",)),
        name="ffn_ln",
    )(x, wg, wu, wd, g, b)


def _rope_table_kernel(pos_ref, freq_ref, sign_ref, cos_ret_ref, sin_ret_ref, cos_mla_ref, sin_mla_ref):
    pos = pos_ref[...].astype(F32)
    ang_ret = pos * freq_ref[0:1, :]
    ang_mla = pos * freq_ref[1:2, :]
    cos_ret_ref[...] = jnp.cos(ang_ret)
    sin_ret_ref[...] = jnp.sin(ang_ret) * sign_ref[...]
    cos_mla_ref[...] = jnp.cos(ang_mla)
    sin_mla_ref[...] = jnp.sin(ang_mla) * sign_ref[...]


def _rope_tables(positions, *, tm):
    n = positions.size
    tm = min(tm, n)
    half_ret = HEAD_DIM // 2
    half_mla = MLA_ROPE // 2
    f_ret = ROPE_THETA ** (-jnp.arange(half_ret, dtype=F32) / half_ret)
    f_mla = ROPE_THETA ** (-jnp.arange(half_mla, dtype=F32) / half_mla)
    freq = jnp.stack([jnp.tile(f_ret, 2), jnp.tile(f_mla, 4)])
    sign = jnp.concatenate([-jnp.ones((1, LANES // 2), F32), jnp.ones((1, LANES // 2), F32)], axis=1)
    tab = jax.ShapeDtypeStruct((n, LANES), F32)
    row = pl.BlockSpec((tm, LANES), lambda i: (i, 0))
    return pl.pallas_call(
        _rope_table_kernel,
        grid=(n // tm,),
        in_specs=[pl.BlockSpec((tm, 1), lambda i: (i, 0)), _resident((2, LANES)), _resident((1, LANES))],
        out_specs=[row, row, row, row],
        out_shape=[tab, tab, tab, tab],
        compiler_params=_params(("arbitrary",)),
        name="rope_tables",
    )(positions.reshape(n, 1), freq, sign)


def _in_proj_kernel(x_ref, w_ref, wgt_ref, gbias_ref, qg_ref, wuq_ref, kvg_ref, wukv_ref,
                    convw_ref, convb_ref, gateb_ref, cr_ref, sr_ref, cm_ref, sm_ref,
                    q_ref, k_ref, v_ref, mq_ref, mk_ref, mv_ref, ig_ref, fg_ref, mo_ref,
                    rq_ref, rk_ref, rv_ref, rg_ref, gate_ref, ext_ref, *, tm):
    xb = x_ref[0].astype(BF16)
    half = LANES // 2

    def proj(c0, c1):
        return _dot(xb, w_ref[:, c0:c1])

    def rope(t, cos, sin):
        return t * cos + pltpu.roll(t, half, axis=1) * sin

    cos_m, sin_m = cm_ref[...], sm_ref[...]
    cos_r, sin_r = cr_ref[...], sr_ref[...]

    cq = _rms_norm(proj(_C_CQ, _C_CKV), qg_ref[...]).astype(BF16)
    qh = _dot(cq, wuq_ref[...])
    sm_scale = (MLA_NOPE + MLA_ROPE) ** -0.5
    pairs = [rope(qh[:, MIX_W + p * LANES:MIX_W + (p + 1) * LANES], cos_m, sin_m) * sm_scale
             for p in range(2)]
    for h in range(N_HEADS):
        q_ref[0, h, :, 0:LANES] = (qh[:, h * LANES:(h + 1) * LANES] * sm_scale).astype(BF16)
        q_ref[0, h, :, LANES:2 * LANES] = pairs[h // 2].astype(BF16)

    ckv = _rms_norm(proj(_C_CKV, _C_KR), kvg_ref[...]).astype(BF16)
    kv = _dot(ckv, wukv_ref[...])
    kr = rope(proj(_C_KR, _C_MQK), cos_m, sin_m)
    lane = lax.broadcasted_iota(jnp.int32, (tm, LANES), 1)
    group_odd = (lane // (MLA_ROPE // 2)) % 2 == 1
    kr_par = [jnp.where(group_odd, 0.0, kr).astype(BF16), jnp.where(group_odd, kr, 0.0).astype(BF16)]
    for h in range(N_HEADS):
        k_ref[0, h, :, 0:LANES] = kv[:, h * LANES:(h + 1) * LANES].astype(BF16)
        k_ref[0, h, :, LANES:2 * LANES] = kr_par[h % 2]
    v_ref[0] = kv[:, MIX_W:2 * MIX_W].astype(BF16)

    @pl.when(pl.program_id(1) == 0)
    def _():
        ext_ref[0:SUBLANES, :] = jnp.zeros((SUBLANES, 2 * MIX_W), F32)

    qk_pre = proj(_C_MQK, _C_MV)
    ext_ref[SUBLANES:SUBLANES + tm, :] = qk_pre
    conv = qk_pre * convw_ref[CONV_K - 1:CONV_K, :] + convb_ref[...]
    for lag in range(1, CONV_K):
        conv = conv + ext_ref[SUBLANES - lag:SUBLANES - lag + tm, :] * convw_ref[CONV_K - 1 - lag:CONV_K - lag, :]
    ext_ref[0:SUBLANES, :] = qk_pre[tm - SUBLANES:tm, :]
    conv = conv * _sigmoid(conv)
    mq_ref[0] = conv[:, 0:MIX_W].astype(BF16)
    mk_ref[0] = (conv[:, MIX_W:2 * MIX_W] * HEAD_DIM ** -0.5).astype(BF16)
    mv_ref[0] = proj(_C_MV, _C_MO).astype(BF16)
    mo_ref[0] = _sigmoid(proj(_C_MO, _C_RQ)).astype(BF16)

    gates_t = _dot_nt(wgt_ref[...], xb) + gbias_ref[...]
    ig_ref[0] = gates_t[0:SUBLANES, :]
    fg_ref[0] = gates_t[SUBLANES:2 * SUBLANES, :]

    rq = proj(_C_RQ, _C_RK)
    rk = proj(_C_RK, _C_RV)
    for h in range(N_HEADS):
        sl = slice(h * LANES, (h + 1) * LANES)
        rq_ref[0, :, sl] = rope(rq[:, sl], cos_r, sin_r).astype(BF16)
        rk_ref[0, :, sl] = (rope(rk[:, sl], cos_r, sin_r) * HEAD_DIM ** -0.5).astype(BF16)
    rv_ref[0] = proj(_C_RV, _C_RG).astype(BF16)
    rg = proj(_C_RG, _C_GL)
    rg_ref[0] = (rg * _sigmoid(rg)).astype(BF16)

    for br in range(N_BRANCH):
        c0 = _C_GL + br * D_MODEL
        sl = slice(br * D_MODEL, (br + 1) * D_MODEL)
        gate_ref[0, :, sl] = _sigmoid(proj(c0, c0 + D_MODEL) + gateb_ref[:, sl]).astype(BF16)


def _in_proj(x, w_main, w_gates_t, gate_bias, q_norm_g, w_uq, kv_norm_g, w_ukv, conv_w, conv_b,
             gate_b, tables, *, tm):
    b, s, _ = x.shape
    tm = min(tm, s)
    nt = s // tm
    cos_r, sin_r, cos_m, sin_m = tables

    def tok(width):
        return pl.BlockSpec((1, tm, width), lambda i, j: (i, j, 0))

    head4 = pl.BlockSpec((1, N_HEADS, tm, MLA_QK), lambda i, j: (i, 0, j, 0))
    gate_t = pl.BlockSpec((1, SUBLANES, tm), lambda i, j: (i, 0, j))
    table = pl.BlockSpec((tm, LANES), lambda i, j: (i * nt + j, 0))
    act = lambda width: jax.ShapeDtypeStruct((b, s, width), BF16)
    qk4 = jax.ShapeDtypeStruct((b, N_HEADS, s, MLA_QK), BF16)
    gt = jax.ShapeDtypeStruct((b, SUBLANES, s), F32)
    return pl.pallas_call(
        functools.partial(_in_proj_kernel, tm=tm),
        grid=(b, nt),
        in_specs=[tok(D_MODEL), _resident(w_main.shape), _resident(w_gates_t.shape),
                  _resident(gate_bias.shape), _resident(q_norm_g.shape), _resident(w_uq.shape),
                  _resident(kv_norm_g.shape), _resident(w_ukv.shape), _resident(conv_w.shape),
                  _resident(conv_b.shape), _resident(gate_b.shape), table, table, table, table],
        out_specs=[head4, head4, tok(MIX_W), tok(MIX_W), tok(MIX_W), tok(MIX_W), gate_t, gate_t,
                   tok(MIX_W), tok(MIX_W), tok(MIX_W), tok(MIX_W), tok(MIX_W), tok(N_BRANCH * D_MODEL)],
        out_shape=[qk4, qk4, act(MIX_W), act(MIX_W), act(MIX_W), act(MIX_W), gt, gt,
                   act(MIX_W), act(MIX_W), act(MIX_W), act(MIX_W), act(MIX_W), act(N_BRANCH * D_MODEL)],
        scratch_shapes=[pltpu.VMEM((tm + SUBLANES, 2 * MIX_W), F32)],
        compiler_params=_params(("arbitrary", "arbitrary")),
        name="in_proj",
    )(x, w_main, w_gates_t, gate_bias, q_norm_g, w_uq, kv_norm_g, w_ukv, conv_w, conv_b, gate_b,
      cos_r, sin_r, cos_m, sin_m)


def _attn_kernel(q_ref, k_ref, v_ref, o_ref, m_ref, l_ref, acc_ref, *, tq):
    i = pl.program_id(2)
    q = q_ref[0, 0]
    m_ref[...] = jnp.full(m_ref.shape, NEG_BIG, F32)
    l_ref[...] = jnp.zeros(l_ref.shape, F32)
    acc_ref[...] = jnp.zeros(acc_ref.shape, F32)

    def step(start, diagonal):
        k = k_ref[0, 0, pl.ds(start, tq), :]
        v = v_ref[0, pl.ds(start, tq), :]
        s = _dot_nt(q, k)
        if diagonal:
            row = lax.broadcasted_iota(jnp.int32, (tq, tq), 0)
            col = lax.broadcasted_iota(jnp.int32, (tq, tq), 1)
            s = jnp.where(col <= row, s, NEG_BIG)
        m_prev = m_ref[...]
        m_new = jnp.maximum(m_prev, jnp.max(s, axis=-1, keepdims=True))
        alpha = jnp.exp(m_prev - m_new)
        p = jnp.exp(s - m_new)
        l_ref[...] = alpha * l_ref[...] + jnp.sum(p, axis=-1, keepdims=True)
        acc_ref[...] = alpha * acc_ref[...] + _dot(p.astype(BF16), v)
        m_ref[...] = m_new

    def body(j, carry):
        step(pl.multiple_of(j * tq, tq), False)
        return carry

    lax.fori_loop(0, i, body, 0)
    step(pl.multiple_of(i * tq, tq), True)
    o_ref[0] = (acc_ref[...] / l_ref[...]).astype(o_ref.dtype)


def _attention(q, k, v, *, tq):
    b, h, s, _ = q.shape
    tq = min(tq, s)
    return pl.pallas_call(
        functools.partial(_attn_kernel, tq=tq),
        grid=(b, h, s // tq),
        in_specs=[pl.BlockSpec((1, 1, tq, MLA_QK), lambda bi, hi, i: (bi, hi, i, 0)),
                  pl.BlockSpec((1, 1, s, MLA_QK), lambda bi, hi, i: (bi, hi, 0, 0)),
                  pl.BlockSpec((1, s, HEAD_DIM), lambda bi, hi, i: (bi, 0, hi))],
        out_specs=pl.BlockSpec((1, tq, HEAD_DIM), lambda bi, hi, i: (bi, i, hi)),
        out_shape=jax.ShapeDtypeStruct((b, s, MIX_W), BF16),
        scratch_shapes=[pltpu.VMEM((tq, 1), F32), pltpu.VMEM((tq, 1), F32),
                        pltpu.VMEM((tq, HEAD_DIM), F32)],
        compiler_params=_params(("arbitrary", "arbitrary", "arbitrary")),
        name="mla_attention",
    )(q, k, v)


def _lane_scan(x, combine, identity):
    lane = lax.broadcasted_iota(jnp.int32, x.shape, 1)
    shift = 1
    while shift < LANES:
        x = combine(x, jnp.where(lane >= shift, pltpu.roll(x, shift, axis=1), identity))
        shift *= 2
    return x


def _head_norm_gate(hh, g_row, gate):
    mu = jnp.mean(hh, axis=-1, keepdims=True)
    hc = hh - mu
    var = jnp.mean(hc * hc, axis=-1, keepdims=True)
    return hc * lax.rsqrt(var + NORM_EPS) * g_row * gate


def _recurrent_kernel(mq_ref, mk_ref, mv_ref, ig_ref, fg_ref, mo_ref, rq_ref, rk_ref, rv_ref, rg_ref,
                      mg_ref, rgn_ref, yml_ref, yrt_ref,
                      c_ref, n_ref, m_ref, r_ref, dec_ref):
    L = CHUNK
    row = lax.broadcasted_iota(jnp.int32, (L, L), 0)
    col = lax.broadcasted_iota(jnp.int32, (L, L), 1)
    causal = col <= row

    @pl.when(pl.program_id(1) == 0)
    def _():
        c_ref[...] = jnp.zeros(c_ref.shape, F32)
        n_ref[...] = jnp.zeros(n_ref.shape, F32)
        m_ref[...] = jnp.zeros(m_ref.shape, F32)
        r_ref[...] = jnp.zeros(r_ref.shape, F32)
        rowf = row.astype(F32)
        colf = col.astype(F32)
        for h in range(N_HEADS):
            lg = math.log(1.0 - 2.0 ** (-5.0 - h))
            dec_ref[h, 0] = jnp.where(causal, jnp.exp(jnp.where(causal, rowf - colf, 0.0) * lg), 0.0)
            dec_ref[h, 1] = jnp.exp((rowf + 1.0) * lg)
            dec_ref[h, 2] = jnp.exp((L - 1.0 - rowf) * lg)

    ig = ig_ref[0]
    fg = fg_ref[0]
    lf = jnp.minimum(fg, 0.0) - jnp.log(1.0 + jnp.exp(-jnp.abs(fg)))
    g = _lane_scan(lf, jnp.add, 0.0)
    g_last = g[:, L - 1:L]
    r = ig - g
    w_loc = g_last + r
    m_loc = jnp.max(w_loc, axis=-1, keepdims=True)
    e_loc = jnp.exp(w_loc - m_loc)
    m_prev = m_ref[:, 0:1]
    mx = jnp.maximum(m_prev, _lane_scan(r, jnp.maximum, NEG_BIG))
    e_inter = jnp.exp(m_prev - mx)
    e_floor = jnp.exp(-g - mx)
    m_new = jnp.maximum(g_last + m_prev, m_loc)
    a_sc = jnp.exp(g_last + m_prev - m_new)
    b_sc = jnp.exp(m_loc - m_new)
    m_ref[...] = jnp.broadcast_to(m_new, m_ref.shape)
    stacked = jnp.concatenate(
        [mx, e_inter, e_floor, e_loc, jnp.zeros((L - 4 * SUBLANES, L), F32)], axis=0)
    cols = stacked.T

    for h in range(N_HEADS):
        sl = slice(h * HEAD_DIM, (h + 1) * HEAD_DIM)
        q = mq_ref[0, :, sl]
        k = mk_ref[0, :, sl]
        v = mv_ref[0, :, sl]
        mx_c = cols[:, h:h + 1]
        e_inter_c = cols[:, SUBLANES + h:SUBLANES + h + 1]
        e_floor_c = cols[:, 2 * SUBLANES + h:2 * SUBLANES + h + 1]
        e_loc_c = cols[:, 3 * SUBLANES + h:3 * SUBLANES + h + 1]
        p = jnp.exp(jnp.where(causal, r[h:h + 1, :] - mx_c, NEG_BIG))
        qk = _dot_nt(q, k) * p
        c_prev = c_ref[h]
        n_prev = n_ref[h:h + 1, :]
        num = _dot(qk.astype(BF16), v) + e_inter_c * _dot(q, c_prev.astype(BF16))
        qf = q.astype(F32)
        den = (jnp.sum(qk, axis=-1, keepdims=True)
               + e_inter_c * jnp.sum(qf * n_prev, axis=-1, keepdims=True))
        hh = num / jnp.maximum(jnp.abs(den), e_floor_c)
        yml_ref[0, :, sl] = _head_norm_gate(hh, mg_ref[:, sl], mo_ref[0, :, sl].astype(F32)).astype(BF16)
        kw = k.astype(F32) * e_loc_c
        a_h = a_sc[h:h + 1, :]
        b_h = b_sc[h:h + 1, :]
        c_ref[h] = a_h * c_prev + b_h * _dot(kw.T.astype(BF16), v)
        n_ref[h:h + 1, :] = a_h * n_prev + b_h * jnp.sum(kw, axis=0, keepdims=True)

        q = rq_ref[0, :, sl]
        k = rk_ref[0, :, sl]
        v = rv_ref[0, :, sl]
        r_prev = r_ref[h]
        intra = _dot((_dot_nt(q, k) * dec_ref[h, 0]).astype(BF16), v)
        inter = dec_ref[h, 1] * _dot(q, r_prev.astype(BF16))
        yrt_ref[0, :, sl] = _head_norm_gate(intra + inter, rgn_ref[:, sl],
                                            rg_ref[0, :, sl].astype(F32)).astype(BF16)
        g_chunk = math.exp(L * math.log(1.0 - 2.0 ** (-5.0 - h)))
        kw = k.astype(F32) * dec_ref[h, 2]
        r_ref[h] = g_chunk * r_prev + _dot(kw.T.astype(BF16), v)


def _recurrent(mq, mk, mv, ig, fg, mo, rq, rk, rv, rg, mlstm_norm_g, ret_norm_g):
    b, s, _ = mq.shape
    tok = pl.BlockSpec((1, CHUNK, MIX_W), lambda i, c: (i, c, 0))
    gate_t = pl.BlockSpec((1, SUBLANES, CHUNK), lambda i, c: (i, 0, c))
    out = jax.ShapeDtypeStruct((b, s, MIX_W), BF16)
    return pl.pallas_call(
        _recurrent_kernel,
        grid=(b, s // CHUNK),
        in_specs=[tok, tok, tok, gate_t, gate_t, tok, tok, tok, tok, tok,
                  _resident((1, MIX_W)), _resident((1, MIX_W))],
        out_specs=[tok, tok],
        out_shape=[out, out],
        scratch_shapes=[pltpu.VMEM((N_HEADS, HEAD_DIM, HEAD_DIM), F32),
                        pltpu.VMEM((SUBLANES, HEAD_DIM), F32),
                        pltpu.VMEM((SUBLANES, LANES), F32),
                        pltpu.VMEM((N_HEADS, HEAD_DIM, HEAD_DIM), F32),
                        pltpu.VMEM((N_HEADS, 3, CHUNK, CHUNK), F32)],
        compiler_params=_params(("arbitrary", "arbitrary")),
        name="recurrent_mixers",
    )(mq, mk, mv, ig, fg, mo, rq, rk, rv, rg, mlstm_norm_g, ret_norm_g)


def _merge_kernel(x_ref, ya_ref, yb_ref, yc_ref, gate_ref, wa_ref, wb_ref, wc_ref, wo_ref, g_ref, b_ref,
                  o_ref, *, alpha):
    merged = None
    for br, (y_ref, w_ref) in enumerate(((ya_ref, wa_ref), (yb_ref, wb_ref), (yc_ref, wc_ref))):
        gate = gate_ref[:, br * D_MODEL:(br + 1) * D_MODEL].astype(F32)
        term = gate * _dot(y_ref[...], w_ref[...])
        merged = term if merged is None else merged + term
    mix = _dot(merged.astype(BF16), wo_ref[...])
    o_ref[...] = _layer_norm(alpha * x_ref[...] + mix, g_ref[...], b_ref[...])


def _merge(x, y_mla, y_ml, y_rt, gate, w_a, w_b, w_c, w_out, g, b, *, alpha, tm):
    n = x.shape[0]
    tm = min(tm, n)
    tok = lambda width: pl.BlockSpec((tm, width), lambda i: (i, 0))
    return pl.pallas_call(
        functools.partial(_merge_kernel, alpha=alpha),
        grid=(n // tm,),
        in_specs=[tok(D_MODEL), tok(MIX_W), tok(MIX_W), tok(MIX_W), tok(N_BRANCH * D_MODEL),
                  _resident((MIX_W, D_MODEL)), _resident((MIX_W, D_MODEL)), _resident((MIX_W, D_MODEL)),
                  _resident((D_MODEL, D_MODEL)), _resident((1, D_MODEL)), _resident((1, D_MODEL))],
        out_specs=tok(D_MODEL),
        out_shape=jax.ShapeDtypeStruct((n, D_MODEL), F32),
        compiler_params=_params(("arbitrary",)),
        name="merge_out_ln",
    )(x, y_mla, y_ml, y_rt, gate, w_a, w_b, w_c, w_out, g, b)


def _pack_w_in(w_in):
    o = 0
    parts = {}
    for name, width in (("cq", MLA_Q_LORA), ("ckv", MLA_KV_LORA), ("kr", MLA_ROPE), ("mq", MIX_W),
                        ("mk", MIX_W), ("mv", MIX_W), ("mi", N_HEADS), ("mf", N_HEADS), ("mo", MIX_W),
                        ("rq", MIX_W), ("rk", MIX_W), ("rv", MIX_W), ("rg", MIX_W),
                        ("gl", N_BRANCH * D_MODEL)):
        parts[name] = w_in[:, o:o + width]
        o += width
    hr = MLA_ROPE // 2
    kr1, kr2 = parts["kr"][:, :hr], parts["kr"][:, hr:]
    main = jnp.concatenate(
        [parts["cq"], parts["ckv"], kr1, kr1, kr2, kr2, parts["mq"], parts["mk"], parts["mv"],
         parts["mo"], parts["rq"], parts["rk"], parts["rv"], parts["rg"], parts["gl"]], axis=1)
    gates_t = jnp.concatenate([parts["mi"], parts["mi"], parts["mf"], parts["mf"]], axis=1).T
    return main.astype(BF16), gates_t.astype(BF16)


def _pack_w_uq(w_uq):
    per = MLA_NOPE + MLA_ROPE
    hr = MLA_ROPE // 2
    nope = [w_uq[:, h * per:h * per + MLA_NOPE] for h in range(N_HEADS)]
    x1 = [w_uq[:, h * per + MLA_NOPE:h * per + MLA_NOPE + hr] for h in range(N_HEADS)]
    x2 = [w_uq[:, h * per + MLA_NOPE + hr:(h + 1) * per] for h in range(N_HEADS)]
    pairs = [x1[0], x1[1], x2[0], x2[1], x1[2], x1[3], x2[2], x2[3]]
    return jnp.concatenate(nope + pairs, axis=1).astype(BF16)


def _pack_w_ukv(w_ukv):
    per = MLA_NOPE + HEAD_DIM
    keys = [w_ukv[:, h * per:h * per + MLA_NOPE] for h in range(N_HEADS)]
    vals = [w_ukv[:, h * per + MLA_NOPE:(h + 1) * per] for h in range(N_HEADS)]
    return jnp.concatenate(keys + vals, axis=1).astype(BF16)


def kernel(x, positions, ffn1_w_gate, ffn1_w_up, ffn1_w_down, ln1_g, ln1_b, w_in, mla_q_norm_g, mla_w_uq, mla_kv_norm_g, mla_w_ukv, mlstm_conv_w, mlstm_conv_b, mlstm_gate_b, mlstm_norm_g, ret_norm_g, w_br_mla, w_br_mlstm, w_br_ret, gate_b, w_out, ln2_g, ln2_b, ffn2_w_gate, ffn2_w_up, ffn2_w_down, ln3_g, ln3_b):
    b, s, d = x.shape
    depth = w_in.shape[0]
    alpha = (2 * depth) ** DEPTH_ALPHA_POW
    n = b * s
    tables = _rope_tables(positions, tm=1024)
    row = lambda a: a.reshape(1, -1)
    ffn = functools.partial(_ffn_ln, alpha=alpha, tm=512, ff_chunk=D_FF // 2)

    h = x.reshape(n, d)
    for l in range(depth):
        h = ffn(h, ffn1_w_gate[l].astype(BF16), ffn1_w_up[l].astype(BF16), ffn1_w_down[l].astype(BF16),
                row(ln1_g[l]), row(ln1_b[l]))

        w_main, w_gates_t = _pack_w_in(w_in[l])
        gbias = mlstm_gate_b[l]
        gate_bias = jnp.concatenate([gbias[:N_HEADS], gbias[:N_HEADS], gbias[N_HEADS:], gbias[N_HEADS:]]
                                    ).reshape(2 * SUBLANES, 1)
        (q, k, v, mq, mk, mv, ig, fg, mo, rq, rk, rv, rg, gate) = _in_proj(
            h.reshape(b, s, d), w_main, w_gates_t, gate_bias, row(mla_q_norm_g[l]), _pack_w_uq(mla_w_uq[l]),
            row(mla_kv_norm_g[l]), _pack_w_ukv(mla_w_ukv[l]), mlstm_conv_w[l], row(mlstm_conv_b[l]),
            row(gate_b[l]), tables, tm=512)
        y_mla = _attention(q, k, v, tq=512)
        y_ml, y_rt = _recurrent(mq, mk, mv, ig, fg, mo, rq, rk, rv, rg,
                                row(mlstm_norm_g[l]), row(ret_norm_g[l]))
        h = _merge(h, y_mla.reshape(n, MIX_W), y_ml.reshape(n, MIX_W), y_rt.reshape(n, MIX_W),
                   gate.reshape(n, N_BRANCH * D_MODEL), w_br_mla[l].astype(BF16), w_br_mlstm[l].astype(BF16),
                   w_br_ret[l].astype(BF16), w_out[l].astype(BF16), row(ln2_g[l]), row(ln2_b[l]),
                   alpha=alpha, tm=512)

        h = ffn(h, ffn2_w_gate[l].astype(BF16), ffn2_w_up[l].astype(BF16), ffn2_w_down[l].astype(BF16),
                row(ln3_g[l]), row(ln3_b[l]))
    return h.reshape(b, s, d)
```

```python
import functools
import math

import jax
import jax.numpy as jnp
from jax import lax
from jax.experimental import pallas as pl
from jax.experimental.pallas import tpu as pltpu

D_MODEL = 1024
D_FF = 2816
N_HEADS = 4
HEAD_DIM = 128
MIX_W = N_HEADS * HEAD_DIM
MLA_Q_LORA = 384
MLA_KV_LORA = 256
MLA_NOPE = 128
MLA_ROPE = 64
MLA_QK = 2 * HEAD_DIM
CONV_K = 4
N_BRANCH = 3
CHUNK = 128
ROPE_THETA = 10000.0
NORM_EPS = 1e-5
NEG_BIG = -1e30
DEPTH_ALPHA_POW = 0.25

LANES = 128
SUBLANES = 8
VMEM_LIMIT_BYTES = 56 * 1024 * 1024

_C_CQ = 0
_C_CKV = _C_CQ + MLA_Q_LORA
_C_KR = _C_CKV + MLA_KV_LORA
_C_MQK = _C_KR + LANES
_C_MV = _C_MQK + 2 * MIX_W
_C_MO = _C_MV + MIX_W
_C_RQ = _C_MO + MIX_W
_C_RK = _C_RQ + MIX_W
_C_RV = _C_RK + MIX_W
_C_RG = _C_RV + MIX_W
_C_GL = _C_RG + MIX_W
_C_END = _C_GL + N_BRANCH * D_MODEL

BF16 = jnp.bfloat16
F32 = jnp.float32


def _resident(shape):
    nd = len(shape)
    return pl.BlockSpec(shape, lambda *_: (0,) * nd, pipeline_mode=pl.Buffered(1))


def _params(sem):
    return pltpu.CompilerParams(dimension_semantics=sem, vmem_limit_bytes=VMEM_LIMIT_BYTES)


def _layer_norm(y, g, b):
    mu = jnp.mean(y, axis=-1, keepdims=True)
    yc = y - mu
    var = jnp.mean(yc * yc, axis=-1, keepdims=True)
    return yc * lax.rsqrt(var + NORM_EPS) * g + b


def _rms_norm(y, g):
    return y * lax.rsqrt(jnp.mean(y * y, axis=-1, keepdims=True) + NORM_EPS) * g


def _sigmoid(x):
    return 1.0 / (1.0 + jnp.exp(-x))


def _dot(a, b):
    return jnp.dot(a, b, preferred_element_type=F32)


def _dot_nt(a, b):
    return lax.dot_general(a, b, (((1,), (1,)), ((), ())), preferred_element_type=F32)


def _ffn_ln_kernel(x_ref, wg_ref, wu_ref, wd_ref, g_ref, b_ref, o_ref, *, alpha, ff_chunk):
    x = x_ref[...]
    xb = x.astype(BF16)
    acc = None
    for c in range(0, D_FF, ff_chunk):
        gate = _dot(xb, wg_ref[:, c:c + ff_chunk])
        up = _dot(xb, wu_ref[:, c:c + ff_chunk])
        act = (gate * _sigmoid(gate) * up).astype(BF16)
        part = _dot(act, wd_ref[c:c + ff_chunk, :])
        acc = part if acc is None else acc + part
    o_ref[...] = _layer_norm(alpha * x + 0.5 * acc, g_ref[...], b_ref[...])


def _ffn_ln(x, wg, wu, wd, g, b, *, alpha, tm, ff_chunk):
    n = x.shape[0]
    tm = min(tm, n)
    row = pl.BlockSpec((tm, D_MODEL), lambda i: (i, 0))
    return pl.pallas_call(
        functools.partial(_ffn_ln_kernel, alpha=alpha, ff_chunk=ff_chunk),
        grid=(n // tm,),
        in_specs=[row, _resident((D_MODEL, D_FF)), _resident((D_MODEL, D_FF)),
                  _resident((D_FF, D_MODEL)), _resident((1, D_MODEL)), _resident((1, D_MODEL))],
        out_specs=row,
        out_shape=jax.ShapeDtypeStruct((n, D_MODEL), F32),
        compiler_params=_params(("arbitrary",)),
        name="ffn_ln",
    )(x, wg, wu, wd, g, b)


def _rope_table_kernel(pos_ref, freq_ref, sign_ref, cos_ret_ref, sin_ret_ref, cos_mla_ref, sin_mla_ref):
    pos = pos_ref[...].astype(F32)
    ang_ret = pos * freq_ref[0:1, :]
    ang_mla = pos * freq_ref[1:2, :]
    cos_ret_ref[...] = jnp.cos(ang_ret)
    sin_ret_ref[...] = jnp.sin(ang_ret) * sign_ref[...]
    cos_mla_ref[...] = jnp.cos(ang_mla)
    sin_mla_ref[...] = jnp.sin(ang_mla) * sign_ref[...]


def _rope_tables(positions, *, tm):
    n = positions.size
    tm = min(tm, n)
    half_ret = HEAD_DIM // 2
    half_mla = MLA_ROPE // 2
    f_ret = ROPE_THETA ** (-jnp.arange(half_ret, dtype=F32) / half_ret)
    f_mla = ROPE_THETA ** (-jnp.arange(half_mla, dtype=F32) / half_mla)
    freq = jnp.stack([jnp.tile(f_ret, 2), jnp.tile(f_mla, 4)])
    sign = jnp.concatenate([-jnp.ones((1, LANES // 2), F32), jnp.ones((1, LANES // 2), F32)], axis=1)
    tab = jax.ShapeDtypeStruct((n, LANES), F32)
    row = pl.BlockSpec((tm, LANES), lambda i: (i, 0))
    return pl.pallas_call(
        _rope_table_kernel,
        grid=(n // tm,),
        in_specs=[pl.BlockSpec((tm, 1), lambda i: (i, 0)), _resident((2, LANES)), _resident((1, LANES))],
        out_specs=[row, row, row, row],
        out_shape=[tab, tab, tab, tab],
        compiler_params=_params(("arbitrary",)),
        name="rope_tables",
    )(positions.reshape(n, 1), freq, sign)


def _in_proj_kernel(x_ref, w_ref, wgt_ref, gbias_ref, qg_ref, wuq_ref, kvg_ref, wukv_ref,
                    convw_ref, convb_ref, gateb_ref, cr_ref, sr_ref, cm_ref, sm_ref,
                    q_ref, k_ref, v_ref, mq_ref, mk_ref, mkt_ref, mv_ref, ig_ref, fg_ref, mo_ref,
                    rq_ref, rk_ref, rkt_ref, rv_ref, rg_ref, gate_ref, ext_ref, *, tm):
    xb = x_ref[0].astype(BF16)
    half = LANES // 2

    def proj(c0, c1):
        return _dot(xb, w_ref[:, c0:c1])

    def rope(t, cos, sin):
        return t * cos + pltpu.roll(t, half, axis=1) * sin

    cos_m, sin_m = cm_ref[...], sm_ref[...]
    cos_r, sin_r = cr_ref[...], sr_ref[...]

    cq = _rms_norm(proj(_C_CQ, _C_CKV), qg_ref[...]).astype(BF16)
    qh = _dot(cq, wuq_ref[...])
    sm_scale = (MLA_NOPE + MLA_ROPE) ** -0.5 * math.log2(math.e)
    pairs = [rope(qh[:, MIX_W + p * LANES:MIX_W + (p + 1) * LANES], cos_m, sin_m) * sm_scale
             for p in range(2)]
    for h in range(N_HEADS):
        q_ref[0, h, :, 0:LANES] = (qh[:, h * LANES:(h + 1) * LANES] * sm_scale).astype(BF16)
        q_ref[0, h, :, LANES:2 * LANES] = pairs[h // 2].astype(BF16)

    ckv = _rms_norm(proj(_C_CKV, _C_KR), kvg_ref[...]).astype(BF16)
    kv = _dot(ckv, wukv_ref[...])
    kr = rope(proj(_C_KR, _C_MQK), cos_m, sin_m)
    lane = lax.broadcasted_iota(jnp.int32, (tm, LANES), 1)
    group_odd = (lane // (MLA_ROPE // 2)) % 2 == 1
    kr_par = [jnp.where(group_odd, 0.0, kr).astype(BF16), jnp.where(group_odd, kr, 0.0).astype(BF16)]
    for h in range(N_HEADS):
        k_ref[0, h, :, 0:LANES] = kv[:, h * LANES:(h + 1) * LANES].astype(BF16)
        k_ref[0, h, :, LANES:2 * LANES] = kr_par[h % 2]
    v_ref[0] = kv[:, MIX_W:2 * MIX_W].astype(BF16)

    @pl.when(pl.program_id(1) == 0)
    def _():
        ext_ref[0:SUBLANES, :] = jnp.zeros((SUBLANES, 2 * MIX_W), F32)

    qk_pre = proj(_C_MQK, _C_MV)
    ext_ref[SUBLANES:SUBLANES + tm, :] = qk_pre
    conv = qk_pre * convw_ref[CONV_K - 1:CONV_K, :] + convb_ref[...]
    for lag in range(1, CONV_K):
        conv = conv + ext_ref[SUBLANES - lag:SUBLANES - lag + tm, :] * convw_ref[CONV_K - 1 - lag:CONV_K - lag, :]
    ext_ref[0:SUBLANES, :] = qk_pre[tm - SUBLANES:tm, :]
    conv = conv * _sigmoid(conv)
    mq_ref[0] = conv[:, 0:MIX_W].astype(BF16)
    mk = conv[:, MIX_W:2 * MIX_W] * HEAD_DIM ** -0.5
    mk_ref[0] = mk.astype(BF16)
    mkt_ref[0] = mk.T.astype(BF16)
    mv_ref[0] = proj(_C_MV, _C_MO).astype(BF16)
    mo_ref[0] = _sigmoid(proj(_C_MO, _C_RQ)).astype(BF16)

    gates_t = _dot_nt(wgt_ref[...], xb) + gbias_ref[...]
    for c in range(tm // CHUNK):
        ig_ref[0, c] = gates_t[0:SUBLANES, c * CHUNK:(c + 1) * CHUNK]
        fg_ref[0, c] = gates_t[SUBLANES:2 * SUBLANES, c * CHUNK:(c + 1) * CHUNK]

    rq = proj(_C_RQ, _C_RK)
    rk = proj(_C_RK, _C_RV)
    for h in range(N_HEADS):
        sl = slice(h * LANES, (h + 1) * LANES)
        rq_ref[0, :, sl] = rope(rq[:, sl], cos_r, sin_r).astype(BF16)
        rk_h = rope(rk[:, sl], cos_r, sin_r) * HEAD_DIM ** -0.5
        rk_ref[0, :, sl] = rk_h.astype(BF16)
        rkt_ref[0, sl, :] = rk_h.T.astype(BF16)
    rv_ref[0] = proj(_C_RV, _C_RG).astype(BF16)
    rg = proj(_C_RG, _C_GL)
    rg_ref[0] = (rg * _sigmoid(rg)).astype(BF16)

    for br in range(N_BRANCH):
        c0 = _C_GL + br * D_MODEL
        sl = slice(br * D_MODEL, (br + 1) * D_MODEL)
        gate_ref[0, :, sl] = _sigmoid(proj(c0, c0 + D_MODEL) + gateb_ref[:, sl]).astype(BF16)


def _in_proj(x, w_main, w_gates_t, gate_bias, q_norm_g, w_uq, kv_norm_g, w_ukv, conv_w, conv_b,
             gate_b, tables, *, tm):
    b, s, _ = x.shape
    tm = min(tm, s)
    nt = s // tm
    cos_r, sin_r, cos_m, sin_m = tables

    def tok(width):
        return pl.BlockSpec((1, tm, width), lambda i, j: (i, j, 0))

    head4 = pl.BlockSpec((1, N_HEADS, tm, MLA_QK), lambda i, j: (i, 0, j, 0))
    gate_t = pl.BlockSpec((1, tm // CHUNK, SUBLANES, CHUNK), lambda i, j: (i, j, 0, 0))
    tok_t = pl.BlockSpec((1, MIX_W, tm), lambda i, j: (i, 0, j))
    table = pl.BlockSpec((tm, LANES), lambda i, j: (i * nt + j, 0))
    act = lambda width: jax.ShapeDtypeStruct((b, s, width), BF16)
    qk4 = jax.ShapeDtypeStruct((b, N_HEADS, s, MLA_QK), BF16)
    gt = jax.ShapeDtypeStruct((b, s // CHUNK, SUBLANES, CHUNK), F32)
    act_t = jax.ShapeDtypeStruct((b, MIX_W, s), BF16)
    return pl.pallas_call(
        functools.partial(_in_proj_kernel, tm=tm),
        grid=(b, nt),
        in_specs=[tok(D_MODEL), _resident(w_main.shape), _resident(w_gates_t.shape),
                  _resident(gate_bias.shape), _resident(q_norm_g.shape), _resident(w_uq.shape),
                  _resident(kv_norm_g.shape), _resident(w_ukv.shape), _resident(conv_w.shape),
                  _resident(conv_b.shape), _resident(gate_b.shape), table, table, table, table],
        out_specs=[head4, head4, tok(MIX_W), tok(MIX_W), tok(MIX_W), tok_t, tok(MIX_W), gate_t, gate_t,
                   tok(MIX_W), tok(MIX_W), tok(MIX_W), tok_t, tok(MIX_W), tok(MIX_W),
                   tok(N_BRANCH * D_MODEL)],
        out_shape=[qk4, qk4, act(MIX_W), act(MIX_W), act(MIX_W), act_t, act(MIX_W), gt, gt,
                   act(MIX_W), act(MIX_W), act(MIX_W), act_t, act(MIX_W), act(MIX_W),
                   act(N_BRANCH * D_MODEL)],
        scratch_shapes=[pltpu.VMEM((tm + SUBLANES, 2 * MIX_W), F32)],
        compiler_params=_params(("arbitrary", "arbitrary")),
        name="in_proj",
    )(x, w_main, w_gates_t, gate_bias, q_norm_g, w_uq, kv_norm_g, w_ukv, conv_w, conv_b, gate_b,
      cos_r, sin_r, cos_m, sin_m)


def _attn_kernel(q_ref, k_ref, v_ref, o_ref, m_ref, acc_ref, *, tq, tk):
    i = pl.program_id(2)
    n_chain = tq // tk
    m_ref[...] = jnp.full(m_ref.shape, NEG_BIG, F32)
    acc_ref[...] = jnp.zeros(acc_ref.shape, F32)
    ones = jnp.ones((tk, HEAD_DIM), BF16)

    def chain_step(c, start, diagonal):
        rows = slice(c * tk, (c + 1) * tk)
        k = k_ref[0, 0, pl.ds(start, tk), :]
        v1 = jnp.concatenate([v_ref[0, pl.ds(start, tk), :], ones], axis=1)
        s = _dot_nt(q_ref[0, 0, rows, :], k)
        if diagonal:
            row = lax.broadcasted_iota(jnp.int32, (tk, tk), 0)
            col = lax.broadcasted_iota(jnp.int32, (tk, tk), 1)
            s = jnp.where(col <= row, s, NEG_BIG)
        blocks = [s[:, j * LANES:(j + 1) * LANES] for j in range(tk // LANES)]
        m_prev = m_ref[rows, :]
        m_blk = functools.reduce(jnp.maximum, blocks)
        m_new = jnp.maximum(m_prev, jnp.max(m_blk, axis=-1, keepdims=True))
        alpha = jnp.exp2(m_prev - m_new)
        p = jnp.concatenate([jnp.exp2(blk - m_new) for blk in blocks], axis=1).astype(BF16)
        acc_ref[rows, :] = jnp.concatenate([alpha, alpha], axis=1) * acc_ref[rows, :] + _dot(p, v1)
        m_ref[rows, :] = m_new

    def body(j, carry):
        start = pl.multiple_of(j * tk, tk)
        for c in range(n_chain):
            chain_step(c, start, False)
        return carry

    lax.fori_loop(0, i * n_chain, body, 0)
    for c in range(n_chain):
        for jj in range(c + 1):
            chain_step(c, pl.multiple_of(i * tq + jj * tk, tk), jj == c)
    acc = acc_ref[...]
    o_ref[0] = (acc[:, 0:HEAD_DIM] / acc[:, HEAD_DIM:2 * HEAD_DIM]).astype(o_ref.dtype)


def _attention(q, k, v, *, tq, tk):
    b, h, s, _ = q.shape
    tq, tk = min(tq, s), min(tk, s)
    return pl.pallas_call(
        functools.partial(_attn_kernel, tq=tq, tk=tk),
        grid=(b, h, s // tq),
        in_specs=[pl.BlockSpec((1, 1, tq, MLA_QK), lambda bi, hi, i: (bi, hi, i, 0)),
                  pl.BlockSpec((1, 1, s, MLA_QK), lambda bi, hi, i: (bi, hi, 0, 0)),
                  pl.BlockSpec((1, s, HEAD_DIM), lambda bi, hi, i: (bi, 0, hi))],
        out_specs=pl.BlockSpec((1, tq, HEAD_DIM), lambda bi, hi, i: (bi, i, hi)),
        out_shape=jax.ShapeDtypeStruct((b, s, MIX_W), BF16),
        scratch_shapes=[pltpu.VMEM((tq, LANES), F32), pltpu.VMEM((tq, 2 * HEAD_DIM), F32)],
        compiler_params=_params(("arbitrary", "arbitrary", "arbitrary")),
        name="mla_attention",
    )(q, k, v)


def _lane_scan(x, combine, identity):
    lane = lax.broadcasted_iota(jnp.int32, x.shape, 1)
    shift = 1
    while shift < LANES:
        x = combine(x, jnp.where(lane >= shift, pltpu.roll(x, shift, axis=1), identity))
        shift *= 2
    return x


def _gate_prep_kernel(ig_ref, fg_ref, r_ref, eloc_ref, a_ref, b_ref, cols_ref, mprev_ref, stack_ref):
    nc = ig_ref.shape[1]
    rows = nc * SUBLANES
    L = CHUNK
    ig = ig_ref[0].reshape(rows, L)
    fg = fg_ref[0].reshape(rows, L)
    lf = jnp.minimum(fg, 0.0) - jnp.log(1.0 + jnp.exp(-jnp.abs(fg)))
    g = _lane_scan(lf, jnp.add, 0.0)
    g_last = jnp.broadcast_to(g[:, L - 1:L], (rows, L))
    r = ig - g
    w_loc = g_last + r
    m_loc = jnp.broadcast_to(jnp.max(w_loc, axis=-1, keepdims=True), (rows, L))
    eloc_ref[0] = jnp.exp(w_loc - m_loc).reshape(nc, SUBLANES, L)
    r_ref[0] = r.reshape(nc, SUBLANES, L)
    cmax = _lane_scan(r, jnp.maximum, NEG_BIG)

    m = jnp.zeros((SUBLANES, L), F32)
    for c in range(nc):
        sl = slice(c * SUBLANES, (c + 1) * SUBLANES)
        mprev_ref[sl, :] = m
        m = jnp.maximum(g_last[sl, :] + m, m_loc[sl, :])
    m_prev = mprev_ref[...]
    mx = jnp.maximum(m_prev, cmax)
    m_new = jnp.maximum(g_last + m_prev, m_loc)
    a_ref[0] = jnp.exp(g_last + m_prev - m_new).reshape(nc, SUBLANES, L)
    b_ref[0] = jnp.exp(m_loc - m_new).reshape(nc, SUBLANES, L)

    stack_ref[...] = jnp.zeros(stack_ref.shape, F32)
    stack_ref[:, 0:SUBLANES, :] = mx.reshape(nc, SUBLANES, L)
    stack_ref[:, SUBLANES:2 * SUBLANES, :] = jnp.exp(m_prev - mx).reshape(nc, SUBLANES, L)
    stack_ref[:, 2 * SUBLANES:3 * SUBLANES, :] = jnp.exp(-g - mx).reshape(nc, SUBLANES, L)

    def body(c, carry):
        cols_ref[0, c] = stack_ref[c].T
        return carry

    lax.fori_loop(0, nc, body, 0)


def _gate_prep(ig, fg):
    b, nc, _, _ = ig.shape
    rows = pl.BlockSpec((1, nc, SUBLANES, CHUNK), lambda i: (i, 0, 0, 0))
    cols = pl.BlockSpec((1, nc, CHUNK, LANES), lambda i: (i, 0, 0, 0))
    row_shape = jax.ShapeDtypeStruct((b, nc, SUBLANES, CHUNK), F32)
    return pl.pallas_call(
        _gate_prep_kernel,
        grid=(b,),
        in_specs=[rows, rows],
        out_specs=[rows, rows, rows, rows, cols],
        out_shape=[row_shape, row_shape, row_shape, row_shape,
                   jax.ShapeDtypeStruct((b, nc, CHUNK, LANES), F32)],
        scratch_shapes=[pltpu.VMEM((nc * SUBLANES, CHUNK), F32), pltpu.VMEM((nc, CHUNK, LANES), F32)],
        compiler_params=_params(("arbitrary",)),
        name="mlstm_gate_prep",
    )(ig, fg)


def _head_norm_gate(hh, g_row, gate):
    mu = jnp.mean(hh, axis=-1, keepdims=True)
    hc = hh - mu
    var = jnp.mean(hc * hc, axis=-1, keepdims=True)
    return hc * lax.rsqrt(var + NORM_EPS) * g_row * gate


def _recurrent_kernel(mq_ref, mk_ref, mkt_ref, mv_ref, r_ref, eloc_ref, a_ref, b_ref, cols_ref, mo_ref,
                      rq_ref, rk_ref, rkt_ref, rv_ref, rg_ref, mg_ref, rgn_ref, yml_ref, yrt_ref,
                      c_ref, r_state_ref, dec_ref):
    L = CHUNK
    row = lax.broadcasted_iota(jnp.int32, (L, L), 0)
    col = lax.broadcasted_iota(jnp.int32, (L, L), 1)
    causal = col <= row

    @pl.when(pl.program_id(1) == 0)
    def _():
        c_ref[...] = jnp.zeros(c_ref.shape, F32)
        r_state_ref[...] = jnp.zeros(r_state_ref.shape, F32)
        rowf = row.astype(F32)
        colf = col.astype(F32)
        for h in range(N_HEADS):
            lg = math.log(1.0 - 2.0 ** (-5.0 - h))
            dec_ref[h, 0] = jnp.where(causal, jnp.exp(jnp.where(causal, rowf - colf, 0.0) * lg), 0.0)
            dec_ref[h, 1] = jnp.exp((rowf + 1.0) * lg)
            dec_ref[h, 2] = jnp.exp((L - 1.0 - colf) * lg)

    r_rows = r_ref[0, 0]
    eloc_rows = eloc_ref[0, 0]
    a_rows = a_ref[0, 0]
    b_rows = b_ref[0, 0]
    cols = cols_ref[0, 0]
    ones = jnp.ones((L, HEAD_DIM), BF16)

    for h in range(N_HEADS):
        sl = slice(h * HEAD_DIM, (h + 1) * HEAD_DIM)
        q = mq_ref[0, :, sl]
        v1 = jnp.concatenate([mv_ref[0, :, sl], ones], axis=1)
        mx_c = cols[:, h:h + 1]
        e_inter_c = cols[:, SUBLANES + h:SUBLANES + h + 1]
        e_floor_c = cols[:, 2 * SUBLANES + h:2 * SUBLANES + h + 1]
        p = jnp.exp(jnp.where(causal, r_rows[h:h + 1, :] - mx_c, NEG_BIG))
        qk = _dot_nt(q, mk_ref[0, :, sl]) * p
        state = c_ref[h]
        nd = _dot(qk.astype(BF16), v1) + e_inter_c * _dot(q, state.astype(BF16))
        hh = nd[:, 0:HEAD_DIM] / jnp.maximum(jnp.abs(nd[:, HEAD_DIM:2 * HEAD_DIM]), e_floor_c)
        yml_ref[0, :, sl] = _head_norm_gate(hh, mg_ref[:, sl], mo_ref[0, :, sl].astype(F32)).astype(BF16)
        kwt = (mkt_ref[0, sl, :].astype(F32) * eloc_rows[h:h + 1, :]).astype(BF16)
        c_ref[h] = a_rows[h:h + 1, 0:1] * state + b_rows[h:h + 1, 0:1] * _dot(kwt, v1)

        q = rq_ref[0, :, sl]
        v = rv_ref[0, :, sl]
        r_prev = r_state_ref[h]
        intra = _dot((_dot_nt(q, rk_ref[0, :, sl]) * dec_ref[h, 0]).astype(BF16), v)
        inter = dec_ref[h, 1] * _dot(q, r_prev.astype(BF16))
        yrt_ref[0, :, sl] = _head_norm_gate(intra + inter, rgn_ref[:, sl],
                                            rg_ref[0, :, sl].astype(F32)).astype(BF16)
        g_chunk = math.exp(L * math.log(1.0 - 2.0 ** (-5.0 - h)))
        kwt = (rkt_ref[0, sl, :].astype(F32) * dec_ref[h, 2]).astype(BF16)
        r_state_ref[h] = g_chunk * r_prev + _dot(kwt, v)


def _recurrent(mq, mk, mkt, mv, r, eloc, a, bsc, cols, mo, rq, rk, rkt, rv, rg, mlstm_norm_g, ret_norm_g):
    b, s, _ = mq.shape
    tok = pl.BlockSpec((1, CHUNK, MIX_W), lambda i, c: (i, c, 0))
    tok_t = pl.BlockSpec((1, MIX_W, CHUNK), lambda i, c: (i, 0, c))
    rows = pl.BlockSpec((1, 1, SUBLANES, CHUNK), lambda i, c: (i, c, 0, 0))
    colsp = pl.BlockSpec((1, 1, CHUNK, LANES), lambda i, c: (i, c, 0, 0))
    out = jax.ShapeDtypeStruct((b, s, MIX_W), BF16)
    return pl.pallas_call(
        _recurrent_kernel,
        grid=(b, s // CHUNK),
        in_specs=[tok, tok, tok_t, tok, rows, rows, rows, rows, colsp, tok,
                  tok, tok, tok_t, tok, tok, _resident((1, MIX_W)), _resident((1, MIX_W))],
        out_specs=[tok, tok],
        out_shape=[out, out],
        scratch_shapes=[pltpu.VMEM((N_HEADS, HEAD_DIM, 2 * HEAD_DIM), F32),
                        pltpu.VMEM((N_HEADS, HEAD_DIM, HEAD_DIM), F32),
                        pltpu.VMEM((N_HEADS, 3, CHUNK, CHUNK), F32)],
        compiler_params=_params(("arbitrary", "arbitrary")),
        name="recurrent_mixers",
    )(mq, mk, mkt, mv, r, eloc, a, bsc, cols, mo, rq, rk, rkt, rv, rg, mlstm_norm_g, ret_norm_g)


def _merge_kernel(x_ref, ya_ref, yb_ref, yc_ref, gate_ref, wa_ref, wb_ref, wc_ref, wo_ref, g_ref, b_ref,
                  o_ref, *, alpha):
    merged = None
    for br, (y_ref, w_ref) in enumerate(((ya_ref, wa_ref), (yb_ref, wb_ref), (yc_ref, wc_ref))):
        gate = gate_ref[:, br * D_MODEL:(br + 1) * D_MODEL].astype(F32)
        term = gate * _dot(y_ref[...], w_ref[...])
        merged = term if merged is None else merged + term
    mix = _dot(merged.astype(BF16), wo_ref[...])
    o_ref[...] = _layer_norm(alpha * x_ref[...] + mix, g_ref[...], b_ref[...])


def _merge(x, y_mla, y_ml, y_rt, gate, w_a, w_b, w_c, w_out, g, b, *, alpha, tm):
    n = x.shape[0]
    tm = min(tm, n)
    tok = lambda width: pl.BlockSpec((tm, width), lambda i: (i, 0))
    return pl.pallas_call(
        functools.partial(_merge_kernel, alpha=alpha),
        grid=(n // tm,),
        in_specs=[tok(D_MODEL), tok(MIX_W), tok(MIX_W), tok(MIX_W), tok(N_BRANCH * D_MODEL),
                  _resident((MIX_W, D_MODEL)), _resident((MIX_W, D_MODEL)), _resident((MIX_W, D_MODEL)),
                  _resident((D_MODEL, D_MODEL)), _resident((1, D_MODEL)), _resident((1, D_MODEL))],
        out_specs=tok(D_MODEL),
        out_shape=jax.ShapeDtypeStruct((n, D_MODEL), F32),
        compiler_params=_params(("arbitrary",)),
        name="merge_out_ln",
    )(x, y_mla, y_ml, y_rt, gate, w_a, w_b, w_c, w_out, g, b)


def _pack_w_in(w_in):
    o = 0
    parts = {}
    for name, width in (("cq", MLA_Q_LORA), ("ckv", MLA_KV_LORA), ("kr", MLA_ROPE), ("mq", MIX_W),
                        ("mk", MIX_W), ("mv", MIX_W), ("mi", N_HEADS), ("mf", N_HEADS), ("mo", MIX_W),
                        ("rq", MIX_W), ("rk", MIX_W), ("rv", MIX_W), ("rg", MIX_W),
                        ("gl", N_BRANCH * D_MODEL)):
        parts[name] = w_in[:, o:o + width]
        o += width
    hr = MLA_ROPE // 2
    kr1, kr2 = parts["kr"][:, :hr], parts["kr"][:, hr:]
    main = jnp.concatenate(
        [parts["cq"], parts["ckv"], kr1, kr1, kr2, kr2, parts["mq"], parts["mk"], parts["mv"],
         parts["mo"], parts["rq"], parts["rk"], parts["rv"], parts["rg"], parts["gl"]], axis=1)
    gates_t = jnp.concatenate([parts["mi"], parts["mi"], parts["mf"], parts["mf"]], axis=1).T
    return main.astype(BF16), gates_t.astype(BF16)


def _pack_w_uq(w_uq):
    per = MLA_NOPE + MLA_ROPE
    hr = MLA_ROPE // 2
    nope = [w_uq[:, h * per:h * per + MLA_NOPE] for h in range(N_HEADS)]
    x1 = [w_uq[:, h * per + MLA_NOPE:h * per + MLA_NOPE + hr] for h in range(N_HEADS)]
    x2 = [w_uq[:, h * per + MLA_NOPE + hr:(h + 1) * per] for h in range(N_HEADS)]
    pairs = [x1[0], x1[1], x2[0], x2[1], x1[2], x1[3], x2[2], x2[3]]
    return jnp.concatenate(nope + pairs, axis=1).astype(BF16)


def _pack_w_ukv(w_ukv):
    per = MLA_NOPE + HEAD_DIM
    keys = [w_ukv[:, h * per:h * per + MLA_NOPE] for h in range(N_HEADS)]
    vals = [w_ukv[:, h * per + MLA_NOPE:(h + 1) * per] for h in range(N_HEADS)]
    return jnp.concatenate(keys + vals, axis=1).astype(BF16)


def kernel(x, positions, ffn1_w_gate, ffn1_w_up, ffn1_w_down, ln1_g, ln1_b, w_in, mla_q_norm_g, mla_w_uq, mla_kv_norm_g, mla_w_ukv, mlstm_conv_w, mlstm_conv_b, mlstm_gate_b, mlstm_norm_g, ret_norm_g, w_br_mla, w_br_mlstm, w_br_ret, gate_b, w_out, ln2_g, ln2_b, ffn2_w_gate, ffn2_w_up, ffn2_w_down, ln3_g, ln3_b):
    b, s, d = x.shape
    depth = w_in.shape[0]
    alpha = (2 * depth) ** DEPTH_ALPHA_POW
    n = b * s
    tables = _rope_tables(positions, tm=1024)
    row = lambda a: a.reshape(1, -1)
    ffn = functools.partial(_ffn_ln, alpha=alpha, tm=512, ff_chunk=D_FF // 2)

    h = x.reshape(n, d)
    for l in range(depth):
        h = ffn(h, ffn1_w_gate[l].astype(BF16), ffn1_w_up[l].astype(BF16), ffn1_w_down[l].astype(BF16),
                row(ln1_g[l]), row(ln1_b[l]))

        w_main, w_gates_t = _pack_w_in(w_in[l])
        gbias = mlstm_gate_b[l]
        gate_bias = jnp.concatenate([gbias[:N_HEADS], gbias[:N_HEADS], gbias[N_HEADS:], gbias[N_HEADS:]]
                                    ).reshape(2 * SUBLANES, 1)
        (q, k, v, mq, mk, mkt, mv, ig, fg, mo, rq, rk, rkt, rv, rg, gate) = _in_proj(
            h.reshape(b, s, d), w_main, w_gates_t, gate_bias, row(mla_q_norm_g[l]), _pack_w_uq(mla_w_uq[l]),
            row(mla_kv_norm_g[l]), _pack_w_ukv(mla_w_ukv[l]), mlstm_conv_w[l], row(mlstm_conv_b[l]),
            row(gate_b[l]), tables, tm=512)
        y_mla = _attention(q, k, v, tq=1024, tk=512)
        r, eloc, a_sc, b_sc, cols = _gate_prep(ig, fg)
        y_ml, y_rt = _recurrent(mq, mk, mkt, mv, r, eloc, a_sc, b_sc, cols, mo, rq, rk, rkt, rv, rg,
                                row(mlstm_norm_g[l]), row(ret_norm_g[l]))
        h = _merge(h, y_mla.reshape(n, MIX_W), y_ml.reshape(n, MIX_W), y_rt.reshape(n, MIX_W),
                   gate.reshape(n, N_BRANCH * D_MODEL), w_br_mla[l].astype(BF16), w_br_mlstm[l].astype(BF16),
                   w_br_ret[l].astype(BF16), w_out[l].astype(BF16), row(ln2_g[l]), row(ln2_b[l]),
                   alpha=alpha, tm=512)

        h = ffn(h, ffn2_w_gate[l].astype(BF16), ffn2_w_up[l].astype(BF16), ffn2_w_down[l].astype(BF16),
                row(ln3_g[l]), row(ln3_b[l]))
    return h.reshape(b, s, d)
```

```python
import functools
import math

import jax
import jax.numpy as jnp
from jax import lax
from jax.experimental import pallas as pl
from jax.experimental.pallas import tpu as pltpu

D_MODEL = 1024
D_FF = 2816
N_HEADS = 4
HEAD_DIM = 128
MIX_W = N_HEADS * HEAD_DIM
MLA_Q_LORA = 384
MLA_KV_LORA = 256
MLA_NOPE = 128
MLA_ROPE = 64
MLA_QK = 2 * HEAD_DIM
CONV_K = 4
N_BRANCH = 3
CHUNK = 128
ROPE_THETA = 10000.0
NORM_EPS = 1e-5
NEG_BIG = -1e30
DEPTH_ALPHA_POW = 0.25

LANES = 128
SUBLANES = 8
MXU_WIDTH = 256
VMEM_LIMIT_BYTES = 56 * 1024 * 1024

_C_CQ = 0
_C_CKV = _C_CQ + MLA_Q_LORA
_C_KR = _C_CKV + MLA_KV_LORA
_C_MQK = _C_KR + LANES
_C_MV = _C_MQK + 2 * MIX_W
_C_MO = _C_MV + MIX_W
_C_RQ = _C_MO + MIX_W
_C_RK = _C_RQ + MIX_W
_C_RV = _C_RK + MIX_W
_C_RG = _C_RV + MIX_W
_C_GL = _C_RG + MIX_W
_C_END = _C_GL + N_BRANCH * D_MODEL

BF16 = jnp.bfloat16
F32 = jnp.float32


def _resident(shape):
    nd = len(shape)
    return pl.BlockSpec(shape, lambda *_: (0,) * nd, pipeline_mode=pl.Buffered(1))


def _params(sem):
    return pltpu.CompilerParams(dimension_semantics=sem, vmem_limit_bytes=VMEM_LIMIT_BYTES)


def _layer_norm(y, g, b):
    mu = jnp.mean(y, axis=-1, keepdims=True)
    yc = y - mu
    var = jnp.mean(yc * yc, axis=-1, keepdims=True)
    return yc * lax.rsqrt(var + NORM_EPS) * g + b


def _rms_norm(y, g):
    return y * lax.rsqrt(jnp.mean(y * y, axis=-1, keepdims=True) + NORM_EPS) * g


def _sigmoid(x):
    return 1.0 / (1.0 + jnp.exp(-x))


def _dot(a, b):
    return jnp.dot(a, b, preferred_element_type=F32)


def _dot_nt(a, b):
    return lax.dot_general(a, b, (((1,), (1,)), ((), ())), preferred_element_type=F32)


def _ffn_ln_kernel(x_ref, wg_ref, wu_ref, wd_ref, g_ref, b_ref, o_ref, *, alpha, sub, ff_chunks):
    for r0 in range(0, x_ref.shape[0], sub):
        rows = slice(r0, r0 + sub)
        x = x_ref[rows, :]
        xb = x.astype(BF16)
        acc = None
        for c0, c1 in ff_chunks:
            gate = _dot(xb, wg_ref[:, c0:c1])
            up = _dot(xb, wu_ref[:, c0:c1])
            act = (gate * _sigmoid(gate) * up).astype(BF16)
            part = _dot(act, wd_ref[c0:c1, :])
            acc = part if acc is None else acc + part
        o_ref[rows, :] = _layer_norm(alpha * x + 0.5 * acc, g_ref[...], b_ref[...])


def _ffn_ln(x, wg, wu, wd, g, b, *, alpha, tm, sub, ff_chunks):
    n = x.shape[0]
    tm = min(tm, n)
    sub = min(sub, tm)
    row = pl.BlockSpec((tm, D_MODEL), lambda i: (i, 0))
    return pl.pallas_call(
        functools.partial(_ffn_ln_kernel, alpha=alpha, sub=sub, ff_chunks=ff_chunks),
        grid=(n // tm,),
        in_specs=[row, _resident((D_MODEL, D_FF)), _resident((D_MODEL, D_FF)),
                  _resident((D_FF, D_MODEL)), _resident((1, D_MODEL)), _resident((1, D_MODEL))],
        out_specs=row,
        out_shape=jax.ShapeDtypeStruct((n, D_MODEL), F32),
        compiler_params=_params(("arbitrary",)),
        name="ffn_ln",
    )(x, wg, wu, wd, g, b)


def _rope_table_kernel(pos_ref, freq_ref, sign_ref, cos_ret_ref, sin_ret_ref, cos_mla_ref, sin_mla_ref):
    pos = pos_ref[...].astype(F32)
    ang_ret = pos * freq_ref[0:1, :]
    ang_mla = pos * freq_ref[1:2, :]
    cos_ret_ref[...] = jnp.cos(ang_ret)
    sin_ret_ref[...] = jnp.sin(ang_ret) * sign_ref[...]
    cos_mla_ref[...] = jnp.cos(ang_mla)
    sin_mla_ref[...] = jnp.sin(ang_mla) * sign_ref[...]


def _rope_tables(positions, *, tm):
    n = positions.size
    tm = min(tm, n)
    half_ret = HEAD_DIM // 2
    half_mla = MLA_ROPE // 2
    f_ret = ROPE_THETA ** (-jnp.arange(half_ret, dtype=F32) / half_ret)
    f_mla = ROPE_THETA ** (-jnp.arange(half_mla, dtype=F32) / half_mla)
    freq = jnp.stack([jnp.tile(f_ret, 2), jnp.tile(f_mla, 4)])
    sign = jnp.concatenate([-jnp.ones((1, LANES // 2), F32), jnp.ones((1, LANES // 2), F32)], axis=1)
    tab = jax.ShapeDtypeStruct((n, LANES), F32)
    row = pl.BlockSpec((tm, LANES), lambda i: (i, 0))
    return pl.pallas_call(
        _rope_table_kernel,
        grid=(n // tm,),
        in_specs=[pl.BlockSpec((tm, 1), lambda i: (i, 0)), _resident((2, LANES)), _resident((1, LANES))],
        out_specs=[row, row, row, row],
        out_shape=[tab, tab, tab, tab],
        compiler_params=_params(("arbitrary",)),
        name="rope_tables",
    )(positions.reshape(n, 1), freq, sign)


def _in_proj_kernel(x_ref, w_ref, wgt_ref, gbias_ref, qg_ref, wuq_ref, kvg_ref, wukv_ref,
                    convw_ref, convb_ref, gateb_ref, cr_ref, sr_ref, cm_ref, sm_ref,
                    q_ref, k_ref, v_ref, mq_ref, mk_ref, mkt_ref, mv_ref, ig_ref, fg_ref, mo_ref,
                    rq_ref, rk_ref, rkt_ref, rv_ref, rg_ref, gate_ref, ext_ref, *, tm):
    xb = x_ref[0].astype(BF16)
    half = LANES // 2

    def proj(c0, c1):
        return _dot(xb, w_ref[:, c0:c1])

    def rope(t, cos, sin):
        return t * cos + pltpu.roll(t, half, axis=1) * sin

    cos_m, sin_m = cm_ref[...], sm_ref[...]
    cos_r, sin_r = cr_ref[...], sr_ref[...]

    cq = _rms_norm(proj(_C_CQ, _C_CKV), qg_ref[...]).astype(BF16)
    qh = _dot(cq, wuq_ref[...])
    sm_scale = (MLA_NOPE + MLA_ROPE) ** -0.5 * math.log2(math.e)
    pairs = [rope(qh[:, MIX_W + p * LANES:MIX_W + (p + 1) * LANES], cos_m, sin_m) * sm_scale
             for p in range(2)]
    for h in range(N_HEADS):
        q_ref[0, h, :, 0:LANES] = (qh[:, h * LANES:(h + 1) * LANES] * sm_scale).astype(BF16)
        q_ref[0, h, :, LANES:2 * LANES] = pairs[h // 2].astype(BF16)

    ckv = _rms_norm(proj(_C_CKV, _C_KR), kvg_ref[...]).astype(BF16)
    kv = _dot(ckv, wukv_ref[...])
    kr = rope(proj(_C_KR, _C_MQK), cos_m, sin_m)
    lane = lax.broadcasted_iota(jnp.int32, (tm, LANES), 1)
    group_odd = (lane // (MLA_ROPE // 2)) % 2 == 1
    kr_par = [jnp.where(group_odd, 0.0, kr).astype(BF16), jnp.where(group_odd, kr, 0.0).astype(BF16)]
    for h in range(N_HEADS):
        k_ref[0, h, :, 0:LANES] = kv[:, h * LANES:(h + 1) * LANES].astype(BF16)
        k_ref[0, h, :, LANES:2 * LANES] = kr_par[h % 2]
    v_ref[0] = kv[:, MIX_W:2 * MIX_W].astype(BF16)

    @pl.when(pl.program_id(1) == 0)
    def _():
        ext_ref[0:SUBLANES, :] = jnp.zeros((SUBLANES, 2 * MIX_W), F32)

    qk_pre = proj(_C_MQK, _C_MV)
    ext_ref[SUBLANES:SUBLANES + tm, :] = qk_pre
    conv = qk_pre * convw_ref[CONV_K - 1:CONV_K, :] + convb_ref[...]
    for lag in range(1, CONV_K):
        conv = conv + ext_ref[SUBLANES - lag:SUBLANES - lag + tm, :] * convw_ref[CONV_K - 1 - lag:CONV_K - lag, :]
    ext_ref[0:SUBLANES, :] = qk_pre[tm - SUBLANES:tm, :]
    conv = conv * _sigmoid(conv)
    mq_ref[0] = conv[:, 0:MIX_W].astype(BF16)
    mk = conv[:, MIX_W:2 * MIX_W] * HEAD_DIM ** -0.5
    mk_ref[0] = mk.astype(BF16)
    mkt_ref[0] = mk.T.astype(BF16)
    mv_ref[0] = proj(_C_MV, _C_MO).astype(BF16)
    mo_ref[0] = _sigmoid(proj(_C_MO, _C_RQ)).astype(BF16)

    gates_t = _dot_nt(wgt_ref[...], xb) + gbias_ref[...]
    for c in range(tm // CHUNK):
        ig_ref[0, c] = gates_t[0:SUBLANES, c * CHUNK:(c + 1) * CHUNK]
        fg_ref[0, c] = gates_t[SUBLANES:2 * SUBLANES, c * CHUNK:(c + 1) * CHUNK]

    rq = proj(_C_RQ, _C_RK)
    rk = proj(_C_RK, _C_RV)
    for h in range(N_HEADS):
        sl = slice(h * LANES, (h + 1) * LANES)
        rq_ref[0, :, sl] = rope(rq[:, sl], cos_r, sin_r).astype(BF16)
        rk_h = rope(rk[:, sl], cos_r, sin_r) * HEAD_DIM ** -0.5
        rk_ref[0, :, sl] = rk_h.astype(BF16)
        rkt_ref[0, sl, :] = rk_h.T.astype(BF16)
    rv_ref[0] = proj(_C_RV, _C_RG).astype(BF16)
    rg = proj(_C_RG, _C_GL)
    rg_ref[0] = (rg * _sigmoid(rg)).astype(BF16)

    for br in range(N_BRANCH):
        c0 = _C_GL + br * D_MODEL
        sl = slice(br * D_MODEL, (br + 1) * D_MODEL)
        gate_ref[0, :, sl] = _sigmoid(proj(c0, c0 + D_MODEL) + gateb_ref[:, sl]).astype(BF16)


def _in_proj(x, w_main, w_gates_t, gate_bias, q_norm_g, w_uq, kv_norm_g, w_ukv, conv_w, conv_b,
             gate_b, tables, *, tm):
    b, s, _ = x.shape
    tm = min(tm, s)
    nt = s // tm
    cos_r, sin_r, cos_m, sin_m = tables

    def tok(width):
        return pl.BlockSpec((1, tm, width), lambda i, j: (i, j, 0))

    head4 = pl.BlockSpec((1, N_HEADS, tm, MLA_QK), lambda i, j: (i, 0, j, 0))
    gate_t = pl.BlockSpec((1, tm // CHUNK, SUBLANES, CHUNK), lambda i, j: (i, j, 0, 0))
    tok_t = pl.BlockSpec((1, MIX_W, tm), lambda i, j: (i, 0, j))
    table = pl.BlockSpec((tm, LANES), lambda i, j: (i * nt + j, 0))
    act = lambda width: jax.ShapeDtypeStruct((b, s, width), BF16)
    qk4 = jax.ShapeDtypeStruct((b, N_HEADS, s, MLA_QK), BF16)
    gt = jax.ShapeDtypeStruct((b, s // CHUNK, SUBLANES, CHUNK), F32)
    act_t = jax.ShapeDtypeStruct((b, MIX_W, s), BF16)
    return pl.pallas_call(
        functools.partial(_in_proj_kernel, tm=tm),
        grid=(b, nt),
        in_specs=[tok(D_MODEL), _resident(w_main.shape), _resident(w_gates_t.shape),
                  _resident(gate_bias.shape), _resident(q_norm_g.shape), _resident(w_uq.shape),
                  _resident(kv_norm_g.shape), _resident(w_ukv.shape), _resident(conv_w.shape),
                  _resident(conv_b.shape), _resident(gate_b.shape), table, table, table, table],
        out_specs=[head4, head4, tok(MIX_W), tok(MIX_W), tok(MIX_W), tok_t, tok(MIX_W), gate_t, gate_t,
                   tok(MIX_W), tok(MIX_W), tok(MIX_W), tok_t, tok(MIX_W), tok(MIX_W),
                   tok(N_BRANCH * D_MODEL)],
        out_shape=[qk4, qk4, act(MIX_W), act(MIX_W), act(MIX_W), act_t, act(MIX_W), gt, gt,
                   act(MIX_W), act(MIX_W), act(MIX_W), act_t, act(MIX_W), act(MIX_W),
                   act(N_BRANCH * D_MODEL)],
        scratch_shapes=[pltpu.VMEM((tm + SUBLANES, 2 * MIX_W), F32)],
        compiler_params=_params(("arbitrary", "arbitrary")),
        name="in_proj",
    )(x, w_main, w_gates_t, gate_bias, q_norm_g, w_uq, kv_norm_g, w_ukv, conv_w, conv_b, gate_b,
      cos_r, sin_r, cos_m, sin_m)


def _attn_kernel(q_ref, k_ref, v_ref, o_ref, m_ref, acc_ref, *, tq, tk):
    i = pl.program_id(2)
    n_chain = tq // tk
    m_ref[...] = jnp.full(m_ref.shape, NEG_BIG, F32)
    acc_ref[...] = jnp.zeros(acc_ref.shape, F32)
    ones = jnp.ones((tk, HEAD_DIM), BF16)

    def chain_step(c, start, diagonal):
        rows = slice(c * tk, (c + 1) * tk)
        k = k_ref[0, 0, pl.ds(start, tk), :]
        v1 = jnp.concatenate([v_ref[0, pl.ds(start, tk), :], ones], axis=1)
        s = _dot_nt(q_ref[0, 0, rows, :], k)
        if diagonal:
            row = lax.broadcasted_iota(jnp.int32, (tk, tk), 0)
            col = lax.broadcasted_iota(jnp.int32, (tk, tk), 1)
            s = jnp.where(col <= row, s, NEG_BIG)
        blocks = [s[:, j * LANES:(j + 1) * LANES] for j in range(tk // LANES)]
        m_prev = m_ref[rows, :]
        m_blk = functools.reduce(jnp.maximum, blocks)
        m_new = jnp.maximum(m_prev, jnp.max(m_blk, axis=-1, keepdims=True))
        alpha = jnp.exp2(m_prev - m_new)
        p = jnp.concatenate([jnp.exp2(blk - m_new) for blk in blocks], axis=1).astype(BF16)
        acc_ref[rows, :] = jnp.concatenate([alpha, alpha], axis=1) * acc_ref[rows, :] + _dot(p, v1)
        m_ref[rows, :] = m_new

    def body(j, carry):
        start = pl.multiple_of(j * tk, tk)
        for c in range(n_chain):
            chain_step(c, start, False)
        return carry

    lax.fori_loop(0, i * n_chain, body, 0)
    for c in range(n_chain):
        for jj in range(c + 1):
            chain_step(c, pl.multiple_of(i * tq + jj * tk, tk), jj == c)
    acc = acc_ref[...]
    o_ref[0] = (acc[:, 0:HEAD_DIM] / acc[:, HEAD_DIM:2 * HEAD_DIM]).astype(o_ref.dtype)


def _attention(q, k, v, *, tq, tk):
    b, h, s, _ = q.shape
    tq, tk = min(tq, s), min(tk, s)
    return pl.pallas_call(
        functools.partial(_attn_kernel, tq=tq, tk=tk),
        grid=(b, h, s // tq),
        in_specs=[pl.BlockSpec((1, 1, tq, MLA_QK), lambda bi, hi, i: (bi, hi, i, 0)),
                  pl.BlockSpec((1, 1, s, MLA_QK), lambda bi, hi, i: (bi, hi, 0, 0)),
                  pl.BlockSpec((1, s, HEAD_DIM), lambda bi, hi, i: (bi, 0, hi))],
        out_specs=pl.BlockSpec((1, tq, HEAD_DIM), lambda bi, hi, i: (bi, i, hi)),
        out_shape=jax.ShapeDtypeStruct((b, s, MIX_W), BF16),
        scratch_shapes=[pltpu.VMEM((tq, LANES), F32), pltpu.VMEM((tq, 2 * HEAD_DIM), F32)],
        compiler_params=_params(("arbitrary", "arbitrary", "arbitrary")),
        name="mla_attention",
    )(q, k, v)


def _lane_scan(x, combine, identity):
    lane = lax.broadcasted_iota(jnp.int32, x.shape, 1)
    shift = 1
    while shift < LANES:
        x = combine(x, jnp.where(lane >= shift, pltpu.roll(x, shift, axis=1), identity))
        shift *= 2
    return x


def _gate_prep_kernel(ig_ref, fg_ref, r_ref, eloc_ref, a_ref, b_ref, cols_ref, mprev_ref, stack_ref):
    nc = ig_ref.shape[1]
    rows = nc * SUBLANES
    L = CHUNK
    ig = ig_ref[0].reshape(rows, L)
    fg = fg_ref[0].reshape(rows, L)
    lf = jnp.minimum(fg, 0.0) - jnp.log(1.0 + jnp.exp(-jnp.abs(fg)))
    g = _lane_scan(lf, jnp.add, 0.0)
    g_last = jnp.broadcast_to(g[:, L - 1:L], (rows, L))
    r = ig - g
    w_loc = g_last + r
    m_loc = jnp.broadcast_to(jnp.max(w_loc, axis=-1, keepdims=True), (rows, L))
    eloc_ref[0] = jnp.exp(w_loc - m_loc).reshape(nc, SUBLANES, L)
    r_ref[0] = r.reshape(nc, SUBLANES, L)
    cmax = _lane_scan(r, jnp.maximum, NEG_BIG)

    m = jnp.zeros((SUBLANES, L), F32)
    for c in range(nc):
        sl = slice(c * SUBLANES, (c + 1) * SUBLANES)
        mprev_ref[sl, :] = m
        m = jnp.maximum(g_last[sl, :] + m, m_loc[sl, :])
    m_prev = mprev_ref[...]
    mx = jnp.maximum(m_prev, cmax)
    m_new = jnp.maximum(g_last + m_prev, m_loc)
    a_ref[0] = jnp.exp(g_last + m_prev - m_new).reshape(nc, SUBLANES, L)
    b_ref[0] = jnp.exp(m_loc - m_new).reshape(nc, SUBLANES, L)

    stack_ref[...] = jnp.zeros(stack_ref.shape, F32)
    stack_ref[:, 0:SUBLANES, :] = mx.reshape(nc, SUBLANES, L)
    stack_ref[:, SUBLANES:2 * SUBLANES, :] = jnp.exp(m_prev - mx).reshape(nc, SUBLANES, L)
    stack_ref[:, 2 * SUBLANES:3 * SUBLANES, :] = jnp.exp(-g - mx).reshape(nc, SUBLANES, L)

    def body(c, carry):
        cols_ref[0, c] = stack_ref[c].T
        return carry

    lax.fori_loop(0, nc, body, 0)


def _gate_prep(ig, fg):
    b, nc, _, _ = ig.shape
    rows = pl.BlockSpec((1, nc, SUBLANES, CHUNK), lambda i: (i, 0, 0, 0))
    cols = pl.BlockSpec((1, nc, CHUNK, LANES), lambda i: (i, 0, 0, 0))
    row_shape = jax.ShapeDtypeStruct((b, nc, SUBLANES, CHUNK), F32)
    return pl.pallas_call(
        _gate_prep_kernel,
        grid=(b,),
        in_specs=[rows, rows],
        out_specs=[rows, rows, rows, rows, cols],
        out_shape=[row_shape, row_shape, row_shape, row_shape,
                   jax.ShapeDtypeStruct((b, nc, CHUNK, LANES), F32)],
        scratch_shapes=[pltpu.VMEM((nc * SUBLANES, CHUNK), F32), pltpu.VMEM((nc, CHUNK, LANES), F32)],
        compiler_params=_params(("arbitrary",)),
        name="mlstm_gate_prep",
    )(ig, fg)


def _head_norm_gate(hh, g_row, gate):
    mu = jnp.mean(hh, axis=-1, keepdims=True)
    hc = hh - mu
    var = jnp.mean(hc * hc, axis=-1, keepdims=True)
    return hc * lax.rsqrt(var + NORM_EPS) * g_row * gate


def _recurrent_kernel(mq_ref, mk_ref, mkt_ref, mv_ref, r_ref, eloc_ref, a_ref, b_ref, cols_ref, mo_ref,
                      rq_ref, rk_ref, rkt_ref, rv_ref, rg_ref, mg_ref, rgn_ref, yml_ref, yrt_ref,
                      c_ref, r_state_ref, dec_ref):
    L = CHUNK
    row = lax.broadcasted_iota(jnp.int32, (L, L), 0)
    col = lax.broadcasted_iota(jnp.int32, (L, L), 1)
    causal = col <= row

    @pl.when(pl.program_id(1) == 0)
    def _():
        c_ref[...] = jnp.zeros(c_ref.shape, F32)
        r_state_ref[...] = jnp.zeros(r_state_ref.shape, F32)
        rowf = row.astype(F32)
        colf = col.astype(F32)
        for h in range(N_HEADS):
            lg = math.log(1.0 - 2.0 ** (-5.0 - h))
            dec_ref[h, 0] = jnp.where(causal, jnp.exp(jnp.where(causal, rowf - colf, 0.0) * lg), 0.0)
            dec_ref[h, 1] = jnp.exp((rowf + 1.0) * lg)
            dec_ref[h, 2] = jnp.exp((L - 1.0 - colf) * lg)

    r_rows = r_ref[0, 0]
    eloc_rows = eloc_ref[0, 0]
    a_rows = a_ref[0, 0]
    b_rows = b_ref[0, 0]
    cols = cols_ref[0, 0]
    ones = jnp.ones((L, HEAD_DIM), BF16)

    for h in range(N_HEADS):
        sl = slice(h * HEAD_DIM, (h + 1) * HEAD_DIM)
        q = mq_ref[0, :, sl]
        v1 = jnp.concatenate([mv_ref[0, :, sl], ones], axis=1)
        mx_c = cols[:, h:h + 1]
        e_inter_c = cols[:, SUBLANES + h:SUBLANES + h + 1]
        e_floor_c = cols[:, 2 * SUBLANES + h:2 * SUBLANES + h + 1]
        p = jnp.exp(jnp.where(causal, r_rows[h:h + 1, :] - mx_c, NEG_BIG))
        qk = _dot_nt(q, mk_ref[0, :, sl]) * p
        state = c_ref[h]
        nd = _dot(qk.astype(BF16), v1) + e_inter_c * _dot(q, state.astype(BF16))
        hh = nd[:, 0:HEAD_DIM] / jnp.maximum(jnp.abs(nd[:, HEAD_DIM:2 * HEAD_DIM]), e_floor_c)
        yml_ref[0, :, sl] = _head_norm_gate(hh, mg_ref[:, sl], mo_ref[0, :, sl].astype(F32)).astype(BF16)
        kwt = (mkt_ref[0, sl, :].astype(F32) * eloc_rows[h:h + 1, :]).astype(BF16)
        c_ref[h] = a_rows[h:h + 1, 0:1] * state + b_rows[h:h + 1, 0:1] * _dot(kwt, v1)

        q = rq_ref[0, :, sl]
        v = rv_ref[0, :, sl]
        r_prev = r_state_ref[h]
        intra = _dot((_dot_nt(q, rk_ref[0, :, sl]) * dec_ref[h, 0]).astype(BF16), v)
        inter = dec_ref[h, 1] * _dot(q, r_prev.astype(BF16))
        yrt_ref[0, :, sl] = _head_norm_gate(intra + inter, rgn_ref[:, sl],
                                            rg_ref[0, :, sl].astype(F32)).astype(BF16)
        g_chunk = math.exp(L * math.log(1.0 - 2.0 ** (-5.0 - h)))
        kwt = (rkt_ref[0, sl, :].astype(F32) * dec_ref[h, 2]).astype(BF16)
        r_state_ref[h] = g_chunk * r_prev + _dot(kwt, v)


def _recurrent(mq, mk, mkt, mv, r, eloc, a, bsc, cols, mo, rq, rk, rkt, rv, rg, mlstm_norm_g, ret_norm_g):
    b, s, _ = mq.shape
    tok = pl.BlockSpec((1, CHUNK, MIX_W), lambda i, c: (i, c, 0))
    tok_t = pl.BlockSpec((1, MIX_W, CHUNK), lambda i, c: (i, 0, c))
    rows = pl.BlockSpec((1, 1, SUBLANES, CHUNK), lambda i, c: (i, c, 0, 0))
    colsp = pl.BlockSpec((1, 1, CHUNK, LANES), lambda i, c: (i, c, 0, 0))
    out = jax.ShapeDtypeStruct((b, s, MIX_W), BF16)
    return pl.pallas_call(
        _recurrent_kernel,
        grid=(b, s // CHUNK),
        in_specs=[tok, tok, tok_t, tok, rows, rows, rows, rows, colsp, tok,
                  tok, tok, tok_t, tok, tok, _resident((1, MIX_W)), _resident((1, MIX_W))],
        out_specs=[tok, tok],
        out_shape=[out, out],
        scratch_shapes=[pltpu.VMEM((N_HEADS, HEAD_DIM, 2 * HEAD_DIM), F32),
                        pltpu.VMEM((N_HEADS, HEAD_DIM, HEAD_DIM), F32),
                        pltpu.VMEM((N_HEADS, 3, CHUNK, CHUNK), F32)],
        compiler_params=_params(("arbitrary", "arbitrary")),
        name="recurrent_mixers",
    )(mq, mk, mkt, mv, r, eloc, a, bsc, cols, mo, rq, rk, rkt, rv, rg, mlstm_norm_g, ret_norm_g)


def _merge_kernel(x_ref, ya_ref, yb_ref, yc_ref, gate_ref, wa_ref, wb_ref, wc_ref, wo_ref, g_ref, b_ref,
                  o_ref, *, alpha):
    merged = None
    for br, (y_ref, w_ref) in enumerate(((ya_ref, wa_ref), (yb_ref, wb_ref), (yc_ref, wc_ref))):
        gate = gate_ref[:, br * D_MODEL:(br + 1) * D_MODEL].astype(F32)
        term = gate * _dot(y_ref[...], w_ref[...])
        merged = term if merged is None else merged + term
    mix = _dot(merged.astype(BF16), wo_ref[...])
    o_ref[...] = _layer_norm(alpha * x_ref[...] + mix, g_ref[...], b_ref[...])


def _merge(x, y_mla, y_ml, y_rt, gate, w_a, w_b, w_c, w_out, g, b, *, alpha, tm):
    n = x.shape[0]
    tm = min(tm, n)
    tok = lambda width: pl.BlockSpec((tm, width), lambda i: (i, 0))
    return pl.pallas_call(
        functools.partial(_merge_kernel, alpha=alpha),
        grid=(n // tm,),
        in_specs=[tok(D_MODEL), tok(MIX_W), tok(MIX_W), tok(MIX_W), tok(N_BRANCH * D_MODEL),
                  _resident((MIX_W, D_MODEL)), _resident((MIX_W, D_MODEL)), _resident((MIX_W, D_MODEL)),
                  _resident((D_MODEL, D_MODEL)), _resident((1, D_MODEL)), _resident((1, D_MODEL))],
        out_specs=tok(D_MODEL),
        out_shape=jax.ShapeDtypeStruct((n, D_MODEL), F32),
        compiler_params=_params(("arbitrary",)),
        name="merge_out_ln",
    )(x, y_mla, y_ml, y_rt, gate, w_a, w_b, w_c, w_out, g, b)


def _pack_w_in(w_in):
    o = 0
    parts = {}
    for name, width in (("cq", MLA_Q_LORA), ("ckv", MLA_KV_LORA), ("kr", MLA_ROPE), ("mq", MIX_W),
                        ("mk", MIX_W), ("mv", MIX_W), ("mi", N_HEADS), ("mf", N_HEADS), ("mo", MIX_W),
                        ("rq", MIX_W), ("rk", MIX_W), ("rv", MIX_W), ("rg", MIX_W),
                        ("gl", N_BRANCH * D_MODEL)):
        parts[name] = w_in[:, o:o + width]
        o += width
    hr = MLA_ROPE // 2
    kr1, kr2 = parts["kr"][:, :hr], parts["kr"][:, hr:]
    main = jnp.concatenate(
        [parts["cq"], parts["ckv"], kr1, kr1, kr2, kr2, parts["mq"], parts["mk"], parts["mv"],
         parts["mo"], parts["rq"], parts["rk"], parts["rv"], parts["rg"], parts["gl"]], axis=1)
    gates_t = jnp.concatenate([parts["mi"], parts["mi"], parts["mf"], parts["mf"]], axis=1).T
    return main.astype(BF16), gates_t.astype(BF16)


def _pack_w_uq(w_uq):
    per = MLA_NOPE + MLA_ROPE
    hr = MLA_ROPE // 2
    nope = [w_uq[:, h * per:h * per + MLA_NOPE] for h in range(N_HEADS)]
    x1 = [w_uq[:, h * per + MLA_NOPE:h * per + MLA_NOPE + hr] for h in range(N_HEADS)]
    x2 = [w_uq[:, h * per + MLA_NOPE + hr:(h + 1) * per] for h in range(N_HEADS)]
    pairs = [x1[0], x1[1], x2[0], x2[1], x1[2], x1[3], x2[2], x2[3]]
    return jnp.concatenate(nope + pairs, axis=1).astype(BF16)


def _pack_w_ukv(w_ukv):
    per = MLA_NOPE + HEAD_DIM
    keys = [w_ukv[:, h * per:h * per + MLA_NOPE] for h in range(N_HEADS)]
    vals = [w_ukv[:, h * per + MLA_NOPE:(h + 1) * per] for h in range(N_HEADS)]
    return jnp.concatenate(keys + vals, axis=1).astype(BF16)


def kernel(x, positions, ffn1_w_gate, ffn1_w_up, ffn1_w_down, ln1_g, ln1_b, w_in, mla_q_norm_g, mla_w_uq, mla_kv_norm_g, mla_w_ukv, mlstm_conv_w, mlstm_conv_b, mlstm_gate_b, mlstm_norm_g, ret_norm_g, w_br_mla, w_br_mlstm, w_br_ret, gate_b, w_out, ln2_g, ln2_b, ffn2_w_gate, ffn2_w_up, ffn2_w_down, ln3_g, ln3_b):
    b, s, d = x.shape
    depth = w_in.shape[0]
    alpha = (2 * depth) ** DEPTH_ALPHA_POW
    n = b * s
    tables = _rope_tables(positions, tm=1024)
    row = lambda a: a.reshape(1, -1)
    ff_split = (D_FF // MXU_WIDTH + 1) // 2 * MXU_WIDTH
    ffn = functools.partial(_ffn_ln, alpha=alpha, tm=1024, sub=512,
                            ff_chunks=((0, ff_split), (ff_split, D_FF)))

    h = x.reshape(n, d)
    for l in range(depth):
        h = ffn(h, ffn1_w_gate[l].astype(BF16), ffn1_w_up[l].astype(BF16), ffn1_w_down[l].astype(BF16),
                row(ln1_g[l]), row(ln1_b[l]))

        w_main, w_gates_t = _pack_w_in(w_in[l])
        gbias = mlstm_gate_b[l]
        gate_bias = jnp.concatenate([gbias[:N_HEADS], gbias[:N_HEADS], gbias[N_HEADS:], gbias[N_HEADS:]]
                                    ).reshape(2 * SUBLANES, 1)
        (q, k, v, mq, mk, mkt, mv, ig, fg, mo, rq, rk, rkt, rv, rg, gate) = _in_proj(
            h.reshape(b, s, d), w_main, w_gates_t, gate_bias, row(mla_q_norm_g[l]), _pack_w_uq(mla_w_uq[l]),
            row(mla_kv_norm_g[l]), _pack_w_ukv(mla_w_ukv[l]), mlstm_conv_w[l], row(mlstm_conv_b[l]),
            row(gate_b[l]), tables, tm=512)
        y_mla = _attention(q, k, v, tq=4096, tk=512)
        r, eloc, a_sc, b_sc, cols = _gate_prep(ig, fg)
        y_ml, y_rt = _recurrent(mq, mk, mkt, mv, r, eloc, a_sc, b_sc, cols, mo, rq, rk, rkt, rv, rg,
                                row(mlstm_norm_g[l]), row(ret_norm_g[l]))
        h = _merge(h, y_mla.reshape(n, MIX_W), y_ml.reshape(n, MIX_W), y_rt.reshape(n, MIX_W),
                   gate.reshape(n, N_BRANCH * D_MODEL), w_br_mla[l].astype(BF16), w_br_mlstm[l].astype(BF16),
                   w_br_ret[l].astype(BF16), w_out[l].astype(BF16), row(ln2_g[l]), row(ln2_b[l]),
                   alpha=alpha, tm=512)

        h = ffn(h, ffn2_w_gate[l].astype(BF16), ffn2_w_up[l].astype(BF16), ffn2_w_down[l].astype(BF16),
                row(ln3_g[l]), row(ln3_b[l]))
    return h.reshape(b, s, d)
```

```python
import functools
import math

import jax
import jax.numpy as jnp
from jax import lax
from jax.experimental import pallas as pl
from jax.experimental.pallas import tpu as pltpu

D_MODEL = 1024
D_FF = 2816
N_HEADS = 4
HEAD_DIM = 128
MIX_W = N_HEADS * HEAD_DIM
MLA_Q_LORA = 384
MLA_KV_LORA = 256
MLA_NOPE = 128
MLA_ROPE = 64
MLA_QK = 2 * HEAD_DIM
CONV_K = 4
N_BRANCH = 3
CHUNK = 128
ROPE_THETA = 10000.0
NORM_EPS = 1e-5
NEG_BIG = -1e30
DEPTH_ALPHA_POW = 0.25

LANES = 128
SUBLANES = 8
MXU_WIDTH = 256
VMEM_LIMIT_BYTES = 56 * 1024 * 1024

_C_CQ = 0
_C_CKV = _C_CQ + MLA_Q_LORA
_C_KR = _C_CKV + MLA_KV_LORA
_C_MQK = _C_KR + LANES
_C_MV = _C_MQK + 2 * MIX_W
_C_MO = _C_MV + MIX_W
_C_RQ = _C_MO + MIX_W
_C_RK = _C_RQ + MIX_W
_C_RV = _C_RK + MIX_W
_C_RG = _C_RV + MIX_W
_C_GL = _C_RG + MIX_W
_C_END = _C_GL + N_BRANCH * D_MODEL

BF16 = jnp.bfloat16
F32 = jnp.float32


def _resident(shape):
    nd = len(shape)
    return pl.BlockSpec(shape, lambda *_: (0,) * nd, pipeline_mode=pl.Buffered(1))


def _params(sem):
    return pltpu.CompilerParams(dimension_semantics=sem, vmem_limit_bytes=VMEM_LIMIT_BYTES)


def _layer_norm(y, g, b):
    mu = jnp.mean(y, axis=-1, keepdims=True)
    yc = y - mu
    var = jnp.mean(yc * yc, axis=-1, keepdims=True)
    return yc * lax.rsqrt(var + NORM_EPS) * g + b


def _rms_norm(y, g):
    return y * lax.rsqrt(jnp.mean(y * y, axis=-1, keepdims=True) + NORM_EPS) * g


def _sigmoid(x):
    return 1.0 / (1.0 + jnp.exp(-x))


def _dot(a, b):
    return jnp.dot(a, b, preferred_element_type=F32)


def _dot_nt(a, b):
    return lax.dot_general(a, b, (((1,), (1,)), ((), ())), preferred_element_type=F32)


FFN_LOAD_CHUNKS = 8


def _stage_weights_bf16(jobs, sem_ref):
    copies = []
    uses = {}
    for src, stage, sem0, dst in jobs:
        slot = uses.get(id(stage), 0) % 2
        uses[id(stage)] = uses.get(id(stage), 0) + 1
        copies.append((pltpu.make_async_copy(src, stage.at[slot], sem_ref.at[sem0 + slot]), stage, slot, dst))
    copies[0][0].start()
    for i, (copy, stage, slot, dst) in enumerate(copies):
        if i + 1 < len(copies):
            copies[i + 1][0].start()
        copy.wait()
        dst[...] = stage[slot].astype(BF16)


def _ffn_ln_kernel(x_ref, wg_hbm, wu_hbm, wd_hbm, g_ref, b_ref, o_ref,
                   wg_ref, wu_ref, wd_ref, stage_up_ref, stage_dn_ref, sem_ref, *, layer, alpha, sub, ff_chunks):
    @pl.when(pl.program_id(0) == 0)
    def _():
        up_rows = D_MODEL // FFN_LOAD_CHUNKS
        dn_rows = D_FF // FFN_LOAD_CHUNKS
        jobs = []
        for w_hbm, w_ref in ((wg_hbm, wg_ref), (wu_hbm, wu_ref)):
            for c in range(FFN_LOAD_CHUNKS):
                rows = pl.ds(c * up_rows, up_rows)
                jobs.append((w_hbm.at[layer, rows, :], stage_up_ref, 0, w_ref.at[rows, :]))
        for c in range(FFN_LOAD_CHUNKS):
            rows = pl.ds(c * dn_rows, dn_rows)
            jobs.append((wd_hbm.at[layer, rows, :], stage_dn_ref, 2, wd_ref.at[rows, :]))
        _stage_weights_bf16(jobs, sem_ref)

    for r0 in range(0, x_ref.shape[0], sub):
        rows = slice(r0, r0 + sub)
        x = x_ref[rows, :]
        xb = x.astype(BF16)
        acc = None
        for c0, c1 in ff_chunks:
            gate = _dot(xb, wg_ref[:, c0:c1])
            up = _dot(xb, wu_ref[:, c0:c1])
            act = (gate * _sigmoid(gate) * up).astype(BF16)
            part = _dot(act, wd_ref[c0:c1, :])
            acc = part if acc is None else acc + part
        o_ref[rows, :] = _layer_norm(alpha * x + 0.5 * acc, g_ref[...], b_ref[...])


def _ffn_ln(x, wg, wu, wd, g, b, *, layer, alpha, tm, sub, ff_chunks):
    n = x.shape[0]
    tm = min(tm, n)
    sub = min(sub, tm)
    row = pl.BlockSpec((tm, D_MODEL), lambda i: (i, 0))
    hbm = pl.BlockSpec(memory_space=pl.ANY)
    return pl.pallas_call(
        functools.partial(_ffn_ln_kernel, layer=layer, alpha=alpha, sub=sub, ff_chunks=ff_chunks),
        grid=(n // tm,),
        in_specs=[row, hbm, hbm, hbm, _resident((1, D_MODEL)), _resident((1, D_MODEL))],
        out_specs=row,
        out_shape=jax.ShapeDtypeStruct((n, D_MODEL), F32),
        scratch_shapes=[pltpu.VMEM((D_MODEL, D_FF), BF16), pltpu.VMEM((D_MODEL, D_FF), BF16),
                        pltpu.VMEM((D_FF, D_MODEL), BF16),
                        pltpu.VMEM((2, D_MODEL // FFN_LOAD_CHUNKS, D_FF), F32),
                        pltpu.VMEM((2, D_FF // FFN_LOAD_CHUNKS, D_MODEL), F32),
                        pltpu.SemaphoreType.DMA((4,))],
        compiler_params=_params(("arbitrary",)),
        name="ffn_ln",
    )(x, wg, wu, wd, g, b)


def _rope_table_kernel(pos_ref, freq_ref, sign_ref, cos_ret_ref, sin_ret_ref, cos_mla_ref, sin_mla_ref):
    pos = pos_ref[...].astype(F32)
    ang_ret = pos * freq_ref[0:1, :]
    ang_mla = pos * freq_ref[1:2, :]
    cos_ret_ref[...] = jnp.cos(ang_ret)
    sin_ret_ref[...] = jnp.sin(ang_ret) * sign_ref[...]
    cos_mla_ref[...] = jnp.cos(ang_mla)
    sin_mla_ref[...] = jnp.sin(ang_mla) * sign_ref[...]


def _rope_tables(positions, *, tm):
    n = positions.size
    tm = min(tm, n)
    half_ret = HEAD_DIM // 2
    half_mla = MLA_ROPE // 2
    f_ret = ROPE_THETA ** (-jnp.arange(half_ret, dtype=F32) / half_ret)
    f_mla = ROPE_THETA ** (-jnp.arange(half_mla, dtype=F32) / half_mla)
    freq = jnp.stack([jnp.tile(f_ret, 2), jnp.tile(f_mla, 4)])
    sign = jnp.concatenate([-jnp.ones((1, LANES // 2), F32), jnp.ones((1, LANES // 2), F32)], axis=1)
    tab = jax.ShapeDtypeStruct((n, LANES), F32)
    row = pl.BlockSpec((tm, LANES), lambda i: (i, 0))
    return pl.pallas_call(
        _rope_table_kernel,
        grid=(n // tm,),
        in_specs=[pl.BlockSpec((tm, 1), lambda i: (i, 0)), _resident((2, LANES)), _resident((1, LANES))],
        out_specs=[row, row, row, row],
        out_shape=[tab, tab, tab, tab],
        compiler_params=_params(("arbitrary",)),
        name="rope_tables",
    )(positions.reshape(n, 1), freq, sign)


def _in_proj_kernel(x_ref, w_ref, wgt_ref, gbias_ref, qg_ref, wuq_ref, kvg_ref, wukv_ref,
                    convw_ref, convb_ref, gateb_ref, cr_ref, sr_ref, cm_ref, sm_ref,
                    q_ref, k_ref, v_ref, mq_ref, mk_ref, mkt_ref, mv_ref, ig_ref, fg_ref, mo_ref,
                    rq_ref, rk_ref, rkt_ref, rv_ref, rg_ref, gate_ref, ext_ref, *, tm):
    xb = x_ref[0].astype(BF16)
    half = LANES // 2

    def proj(c0, c1):
        return _dot(xb, w_ref[:, c0:c1])

    def rope(t, cos, sin):
        return t * cos + pltpu.roll(t, half, axis=1) * sin

    cos_m, sin_m = cm_ref[...], sm_ref[...]
    cos_r, sin_r = cr_ref[...], sr_ref[...]

    cq = _rms_norm(proj(_C_CQ, _C_CKV), qg_ref[...]).astype(BF16)
    qh = _dot(cq, wuq_ref[...])
    sm_scale = (MLA_NOPE + MLA_ROPE) ** -0.5 * math.log2(math.e)
    pairs = [rope(qh[:, MIX_W + p * LANES:MIX_W + (p + 1) * LANES], cos_m, sin_m) * sm_scale
             for p in range(2)]
    for h in range(N_HEADS):
        q_ref[0, h, :, 0:LANES] = (qh[:, h * LANES:(h + 1) * LANES] * sm_scale).astype(BF16)
        q_ref[0, h, :, LANES:2 * LANES] = pairs[h // 2].astype(BF16)

    ckv = _rms_norm(proj(_C_CKV, _C_KR), kvg_ref[...]).astype(BF16)
    kv = _dot(ckv, wukv_ref[...])
    kr = rope(proj(_C_KR, _C_MQK), cos_m, sin_m)
    lane = lax.broadcasted_iota(jnp.int32, (tm, LANES), 1)
    group_odd = (lane // (MLA_ROPE // 2)) % 2 == 1
    kr_par = [jnp.where(group_odd, 0.0, kr).astype(BF16), jnp.where(group_odd, kr, 0.0).astype(BF16)]
    for h in range(N_HEADS):
        k_ref[0, h, :, 0:LANES] = kv[:, h * LANES:(h + 1) * LANES].astype(BF16)
        k_ref[0, h, :, LANES:2 * LANES] = kr_par[h % 2]
    v_ref[0] = kv[:, MIX_W:2 * MIX_W].astype(BF16)

    @pl.when(pl.program_id(1) == 0)
    def _():
        ext_ref[0:SUBLANES, :] = jnp.zeros((SUBLANES, 2 * MIX_W), F32)

    qk_pre = proj(_C_MQK, _C_MV)
    ext_ref[SUBLANES:SUBLANES + tm, :] = qk_pre
    conv = qk_pre * convw_ref[CONV_K - 1:CONV_K, :] + convb_ref[...]
    for lag in range(1, CONV_K):
        conv = conv + ext_ref[SUBLANES - lag:SUBLANES - lag + tm, :] * convw_ref[CONV_K - 1 - lag:CONV_K - lag, :]
    ext_ref[0:SUBLANES, :] = qk_pre[tm - SUBLANES:tm, :]
    conv = conv * _sigmoid(conv)
    mq_ref[0] = conv[:, 0:MIX_W].astype(BF16)
    mk = conv[:, MIX_W:2 * MIX_W] * HEAD_DIM ** -0.5
    mk_ref[0] = mk.astype(BF16)
    mkt_ref[0] = mk.T.astype(BF16)
    mv_ref[0] = proj(_C_MV, _C_MO).astype(BF16)
    mo_ref[0] = _sigmoid(proj(_C_MO, _C_RQ)).astype(BF16)

    gates_t = _dot_nt(wgt_ref[...], xb) + gbias_ref[...]
    for c in range(tm // CHUNK):
        ig_ref[0, c] = gates_t[0:SUBLANES, c * CHUNK:(c + 1) * CHUNK]
        fg_ref[0, c] = gates_t[SUBLANES:2 * SUBLANES, c * CHUNK:(c + 1) * CHUNK]

    rq = proj(_C_RQ, _C_RK)
    rk = proj(_C_RK, _C_RV)
    for h in range(N_HEADS):
        sl = slice(h * LANES, (h + 1) * LANES)
        rq_ref[0, :, sl] = rope(rq[:, sl], cos_r, sin_r).astype(BF16)
        rk_h = rope(rk[:, sl], cos_r, sin_r) * HEAD_DIM ** -0.5
        rk_ref[0, :, sl] = rk_h.astype(BF16)
        rkt_ref[0, sl, :] = rk_h.T.astype(BF16)
    rv_ref[0] = proj(_C_RV, _C_RG).astype(BF16)
    rg = proj(_C_RG, _C_GL)
    rg_ref[0] = (rg * _sigmoid(rg)).astype(BF16)

    for br in range(N_BRANCH):
        c0 = _C_GL + br * D_MODEL
        sl = slice(br * D_MODEL, (br + 1) * D_MODEL)
        gate_ref[0, :, sl] = _sigmoid(proj(c0, c0 + D_MODEL) + gateb_ref[:, sl]).astype(BF16)


def _in_proj(x, w_main, w_gates_t, gate_bias, q_norm_g, w_uq, kv_norm_g, w_ukv, conv_w, conv_b,
             gate_b, tables, *, tm):
    b, s, _ = x.shape
    tm = min(tm, s)
    nt = s // tm
    cos_r, sin_r, cos_m, sin_m = tables

    def tok(width):
        return pl.BlockSpec((1, tm, width), lambda i, j: (i, j, 0))

    head4 = pl.BlockSpec((1, N_HEADS, tm, MLA_QK), lambda i, j: (i, 0, j, 0))
    gate_t = pl.BlockSpec((1, tm // CHUNK, SUBLANES, CHUNK), lambda i, j: (i, j, 0, 0))
    tok_t = pl.BlockSpec((1, MIX_W, tm), lambda i, j: (i, 0, j))
    table = pl.BlockSpec((tm, LANES), lambda i, j: (i * nt + j, 0))
    act = lambda width: jax.ShapeDtypeStruct((b, s, width), BF16)
    qk4 = jax.ShapeDtypeStruct((b, N_HEADS, s, MLA_QK), BF16)
    gt = jax.ShapeDtypeStruct((b, s // CHUNK, SUBLANES, CHUNK), F32)
    act_t = jax.ShapeDtypeStruct((b, MIX_W, s), BF16)
    return pl.pallas_call(
        functools.partial(_in_proj_kernel, tm=tm),
        grid=(b, nt),
        in_specs=[tok(D_MODEL), _resident(w_main.shape), _resident(w_gates_t.shape),
                  _resident(gate_bias.shape), _resident(q_norm_g.shape), _resident(w_uq.shape),
                  _resident(kv_norm_g.shape), _resident(w_ukv.shape), _resident(conv_w.shape),
                  _resident(conv_b.shape), _resident(gate_b.shape), table, table, table, table],
        out_specs=[head4, head4, tok(MIX_W), tok(MIX_W), tok(MIX_W), tok_t, tok(MIX_W), gate_t, gate_t,
                   tok(MIX_W), tok(MIX_W), tok(MIX_W), tok_t, tok(MIX_W), tok(MIX_W),
                   tok(N_BRANCH * D_MODEL)],
        out_shape=[qk4, qk4, act(MIX_W), act(MIX_W), act(MIX_W), act_t, act(MIX_W), gt, gt,
                   act(MIX_W), act(MIX_W), act(MIX_W), act_t, act(MIX_W), act(MIX_W),
                   act(N_BRANCH * D_MODEL)],
        scratch_shapes=[pltpu.VMEM((tm + SUBLANES, 2 * MIX_W), F32)],
        compiler_params=_params(("arbitrary", "arbitrary")),
        name="in_proj",
    )(x, w_main, w_gates_t, gate_bias, q_norm_g, w_uq, kv_norm_g, w_ukv, conv_w, conv_b, gate_b,
      cos_r, sin_r, cos_m, sin_m)


def _attn_kernel(q_ref, k_ref, v_ref, o_ref, m_ref, acc_ref, *, tq, tk):
    i = pl.program_id(2)
    n_chain = tq // tk
    m_ref[...] = jnp.full(m_ref.shape, NEG_BIG, F32)
    acc_ref[...] = jnp.zeros(acc_ref.shape, F32)
    ones = jnp.ones((tk, HEAD_DIM), BF16)

    def chain_step(c, start, diagonal):
        rows = slice(c * tk, (c + 1) * tk)
        k = k_ref[0, 0, pl.ds(start, tk), :]
        v1 = jnp.concatenate([v_ref[0, pl.ds(start, tk), :], ones], axis=1)
        s = _dot_nt(q_ref[0, 0, rows, :], k)
        if diagonal:
            row = lax.broadcasted_iota(jnp.int32, (tk, tk), 0)
            col = lax.broadcasted_iota(jnp.int32, (tk, tk), 1)
            s = jnp.where(col <= row, s, NEG_BIG)
        blocks = [s[:, j * LANES:(j + 1) * LANES] for j in range(tk // LANES)]
        m_prev = m_ref[rows, :]
        m_blk = functools.reduce(jnp.maximum, blocks)
        m_new = jnp.maximum(m_prev, jnp.max(m_blk, axis=-1, keepdims=True))
        alpha = jnp.exp2(m_prev - m_new)
        p = jnp.concatenate([jnp.exp2(blk - m_new) for blk in blocks], axis=1).astype(BF16)
        acc_ref[rows, :] = jnp.concatenate([alpha, alpha], axis=1) * acc_ref[rows, :] + _dot(p, v1)
        m_ref[rows, :] = m_new

    def body(j, carry):
        start = pl.multiple_of(j * tk, tk)
        for c in range(n_chain):
            chain_step(c, start, False)
        return carry

    lax.fori_loop(0, i * n_chain, body, 0)
    for c in range(n_chain):
        for jj in range(c + 1):
            chain_step(c, pl.multiple_of(i * tq + jj * tk, tk), jj == c)
    acc = acc_ref[...]
    o_ref[0] = (acc[:, 0:HEAD_DIM] / acc[:, HEAD_DIM:2 * HEAD_DIM]).astype(o_ref.dtype)


def _attention(q, k, v, *, tq, tk):
    b, h, s, _ = q.shape
    tq, tk = min(tq, s), min(tk, s)
    return pl.pallas_call(
        functools.partial(_attn_kernel, tq=tq, tk=tk),
        grid=(b, h, s // tq),
        in_specs=[pl.BlockSpec((1, 1, tq, MLA_QK), lambda bi, hi, i: (bi, hi, i, 0)),
                  pl.BlockSpec((1, 1, s, MLA_QK), lambda bi, hi, i: (bi, hi, 0, 0)),
                  pl.BlockSpec((1, s, HEAD_DIM), lambda bi, hi, i: (bi, 0, hi))],
        out_specs=pl.BlockSpec((1, tq, HEAD_DIM), lambda bi, hi, i: (bi, i, hi)),
        out_shape=jax.ShapeDtypeStruct((b, s, MIX_W), BF16),
        scratch_shapes=[pltpu.VMEM((tq, LANES), F32), pltpu.VMEM((tq, 2 * HEAD_DIM), F32)],
        compiler_params=_params(("arbitrary", "arbitrary", "arbitrary")),
        name="mla_attention",
    )(q, k, v)


def _lane_scan(x, combine, identity):
    lane = lax.broadcasted_iota(jnp.int32, x.shape, 1)
    shift = 1
    while shift < LANES:
        x = combine(x, jnp.where(lane >= shift, pltpu.roll(x, shift, axis=1), identity))
        shift *= 2
    return x


def _gate_prep_kernel(ig_ref, fg_ref, r_ref, eloc_ref, a_ref, b_ref, cols_ref, mprev_ref, stack_ref):
    nc = ig_ref.shape[1]
    rows = nc * SUBLANES
    L = CHUNK
    ig = ig_ref[0].reshape(rows, L)
    fg = fg_ref[0].reshape(rows, L)
    lf = jnp.minimum(fg, 0.0) - jnp.log(1.0 + jnp.exp(-jnp.abs(fg)))
    g = _lane_scan(lf, jnp.add, 0.0)
    g_last = jnp.broadcast_to(g[:, L - 1:L], (rows, L))
    r = ig - g
    w_loc = g_last + r
    m_loc = jnp.broadcast_to(jnp.max(w_loc, axis=-1, keepdims=True), (rows, L))
    eloc_ref[0] = jnp.exp(w_loc - m_loc).reshape(nc, SUBLANES, L)
    r_ref[0] = r.reshape(nc, SUBLANES, L)
    cmax = _lane_scan(r, jnp.maximum, NEG_BIG)

    m = jnp.zeros((SUBLANES, L), F32)
    for c in range(nc):
        sl = slice(c * SUBLANES, (c + 1) * SUBLANES)
        mprev_ref[sl, :] = m
        m = jnp.maximum(g_last[sl, :] + m, m_loc[sl, :])
    m_prev = mprev_ref[...]
    mx = jnp.maximum(m_prev, cmax)
    m_new = jnp.maximum(g_last + m_prev, m_loc)
    a_ref[0] = jnp.exp(g_last + m_prev - m_new).reshape(nc, SUBLANES, L)
    b_ref[0] = jnp.exp(m_loc - m_new).reshape(nc, SUBLANES, L)

    stack_ref[...] = jnp.zeros(stack_ref.shape, F32)
    stack_ref[:, 0:SUBLANES, :] = mx.reshape(nc, SUBLANES, L)
    stack_ref[:, SUBLANES:2 * SUBLANES, :] = jnp.exp(m_prev - mx).reshape(nc, SUBLANES, L)
    stack_ref[:, 2 * SUBLANES:3 * SUBLANES, :] = jnp.exp(-g - mx).reshape(nc, SUBLANES, L)

    def body(c, carry):
        cols_ref[0, c] = stack_ref[c].T
        return carry

    lax.fori_loop(0, nc, body, 0)


def _gate_prep(ig, fg):
    b, nc, _, _ = ig.shape
    rows = pl.BlockSpec((1, nc, SUBLANES, CHUNK), lambda i: (i, 0, 0, 0))
    cols = pl.BlockSpec((1, nc, CHUNK, LANES), lambda i: (i, 0, 0, 0))
    row_shape = jax.ShapeDtypeStruct((b, nc, SUBLANES, CHUNK), F32)
    return pl.pallas_call(
        _gate_prep_kernel,
        grid=(b,),
        in_specs=[rows, rows],
        out_specs=[rows, rows, rows, rows, cols],
        out_shape=[row_shape, row_shape, row_shape, row_shape,
                   jax.ShapeDtypeStruct((b, nc, CHUNK, LANES), F32)],
        scratch_shapes=[pltpu.VMEM((nc * SUBLANES, CHUNK), F32), pltpu.VMEM((nc, CHUNK, LANES), F32)],
        compiler_params=_params(("arbitrary",)),
        name="mlstm_gate_prep",
    )(ig, fg)


def _head_norm_gate(hh, g_row, gate):
    mu = jnp.mean(hh, axis=-1, keepdims=True)
    hc = hh - mu
    var = jnp.mean(hc * hc, axis=-1, keepdims=True)
    return hc * lax.rsqrt(var + NORM_EPS) * g_row * gate


def _recurrent_kernel(mq_ref, mk_ref, mkt_ref, mv_ref, r_ref, eloc_ref, a_ref, b_ref, cols_ref, mo_ref,
                      rq_ref, rk_ref, rkt_ref, rv_ref, rg_ref, mg_ref, rgn_ref, yml_ref, yrt_ref,
                      c_ref, r_state_ref, dec_ref):
    L = CHUNK
    row = lax.broadcasted_iota(jnp.int32, (L, L), 0)
    col = lax.broadcasted_iota(jnp.int32, (L, L), 1)
    causal = col <= row

    @pl.when(pl.program_id(0) == 0)
    def _():
        c_ref[...] = jnp.zeros(c_ref.shape, F32)
        r_state_ref[...] = jnp.zeros(r_state_ref.shape, F32)
        rowf = row.astype(F32)
        colf = col.astype(F32)
        for h in range(N_HEADS):
            lg = math.log(1.0 - 2.0 ** (-5.0 - h))
            dec_ref[h, 0] = jnp.where(causal, jnp.exp(jnp.where(causal, rowf - colf, 0.0) * lg), 0.0)
            dec_ref[h, 1] = jnp.exp((rowf + 1.0) * lg)
            dec_ref[h, 2] = jnp.exp((L - 1.0 - colf) * lg)

    ones = jnp.ones((L, HEAD_DIM), BF16)
    for bi in range(mq_ref.shape[0]):
        r_rows = r_ref[bi, 0]
        eloc_rows = eloc_ref[bi, 0]
        a_rows = a_ref[bi, 0]
        b_rows = b_ref[bi, 0]
        cols = cols_ref[bi, 0]
        for h in range(N_HEADS):
            sl = slice(h * HEAD_DIM, (h + 1) * HEAD_DIM)
            st = bi * N_HEADS + h
            q = mq_ref[bi, :, sl]
            v1 = jnp.concatenate([mv_ref[bi, :, sl], ones], axis=1)
            mx_c = cols[:, h:h + 1]
            e_inter_c = cols[:, SUBLANES + h:SUBLANES + h + 1]
            e_floor_c = cols[:, 2 * SUBLANES + h:2 * SUBLANES + h + 1]
            p = jnp.exp(jnp.where(causal, r_rows[h:h + 1, :] - mx_c, NEG_BIG))
            qk = _dot_nt(q, mk_ref[bi, :, sl]) * p
            state = c_ref[st]
            nd = _dot(qk.astype(BF16), v1) + e_inter_c * _dot(q, state.astype(BF16))
            hh = nd[:, 0:HEAD_DIM] / jnp.maximum(jnp.abs(nd[:, HEAD_DIM:2 * HEAD_DIM]), e_floor_c)
            yml_ref[bi, :, sl] = _head_norm_gate(hh, mg_ref[:, sl], mo_ref[bi, :, sl].astype(F32)).astype(BF16)
            kwt = (mkt_ref[bi, sl, :].astype(F32) * eloc_rows[h:h + 1, :]).astype(BF16)
            c_ref[st] = a_rows[h:h + 1, 0:1] * state + b_rows[h:h + 1, 0:1] * _dot(kwt, v1)

            q = rq_ref[bi, :, sl]
            v = rv_ref[bi, :, sl]
            r_prev = r_state_ref[st]
            intra = _dot((_dot_nt(q, rk_ref[bi, :, sl]) * dec_ref[h, 0]).astype(BF16), v)
            inter = dec_ref[h, 1] * _dot(q, r_prev.astype(BF16))
            yrt_ref[bi, :, sl] = _head_norm_gate(intra + inter, rgn_ref[:, sl],
                                                 rg_ref[bi, :, sl].astype(F32)).astype(BF16)
            g_chunk = math.exp(L * math.log(1.0 - 2.0 ** (-5.0 - h)))
            kwt = (rkt_ref[bi, sl, :].astype(F32) * dec_ref[h, 2]).astype(BF16)
            r_state_ref[st] = g_chunk * r_prev + _dot(kwt, v)


def _recurrent(mq, mk, mkt, mv, r, eloc, a, bsc, cols, mo, rq, rk, rkt, rv, rg, mlstm_norm_g, ret_norm_g):
    b, s, _ = mq.shape
    tok = pl.BlockSpec((b, CHUNK, MIX_W), lambda c: (0, c, 0))
    tok_t = pl.BlockSpec((b, MIX_W, CHUNK), lambda c: (0, 0, c))
    rows = pl.BlockSpec((b, 1, SUBLANES, CHUNK), lambda c: (0, c, 0, 0))
    colsp = pl.BlockSpec((b, 1, CHUNK, LANES), lambda c: (0, c, 0, 0))
    out = jax.ShapeDtypeStruct((b, s, MIX_W), BF16)
    return pl.pallas_call(
        _recurrent_kernel,
        grid=(s // CHUNK,),
        in_specs=[tok, tok, tok_t, tok, rows, rows, rows, rows, colsp, tok,
                  tok, tok, tok_t, tok, tok, _resident((1, MIX_W)), _resident((1, MIX_W))],
        out_specs=[tok, tok],
        out_shape=[out, out],
        scratch_shapes=[pltpu.VMEM((b * N_HEADS, HEAD_DIM, 2 * HEAD_DIM), F32),
                        pltpu.VMEM((b * N_HEADS, HEAD_DIM, HEAD_DIM), F32),
                        pltpu.VMEM((N_HEADS, 3, CHUNK, CHUNK), F32)],
        compiler_params=_params(("arbitrary",)),
        name="recurrent_mixers",
    )(mq, mk, mkt, mv, r, eloc, a, bsc, cols, mo, rq, rk, rkt, rv, rg, mlstm_norm_g, ret_norm_g)


def _merge_kernel(x_ref, ya_ref, yb_ref, yc_ref, gate_ref, wa_ref, wb_ref, wc_ref, wo_ref, g_ref, b_ref,
                  o_ref, *, alpha):
    merged = None
    for br, (y_ref, w_ref) in enumerate(((ya_ref, wa_ref), (yb_ref, wb_ref), (yc_ref, wc_ref))):
        gate = gate_ref[:, br * D_MODEL:(br + 1) * D_MODEL].astype(F32)
        term = gate * _dot(y_ref[...], w_ref[...])
        merged = term if merged is None else merged + term
    mix = _dot(merged.astype(BF16), wo_ref[...])
    o_ref[...] = _layer_norm(alpha * x_ref[...] + mix, g_ref[...], b_ref[...])


def _merge(x, y_mla, y_ml, y_rt, gate, w_a, w_b, w_c, w_out, g, b, *, alpha, tm):
    n = x.shape[0]
    tm = min(tm, n)
    tok = lambda width: pl.BlockSpec((tm, width), lambda i: (i, 0))
    return pl.pallas_call(
        functools.partial(_merge_kernel, alpha=alpha),
        grid=(n // tm,),
        in_specs=[tok(D_MODEL), tok(MIX_W), tok(MIX_W), tok(MIX_W), tok(N_BRANCH * D_MODEL),
                  _resident((MIX_W, D_MODEL)), _resident((MIX_W, D_MODEL)), _resident((MIX_W, D_MODEL)),
                  _resident((D_MODEL, D_MODEL)), _resident((1, D_MODEL)), _resident((1, D_MODEL))],
        out_specs=tok(D_MODEL),
        out_shape=jax.ShapeDtypeStruct((n, D_MODEL), F32),
        compiler_params=_params(("arbitrary",)),
        name="merge_out_ln",
    )(x, y_mla, y_ml, y_rt, gate, w_a, w_b, w_c, w_out, g, b)


def _pack_w_in(w_in):
    o = 0
    parts = {}
    for name, width in (("cq", MLA_Q_LORA), ("ckv", MLA_KV_LORA), ("kr", MLA_ROPE), ("mq", MIX_W),
                        ("mk", MIX_W), ("mv", MIX_W), ("mi", N_HEADS), ("mf", N_HEADS), ("mo", MIX_W),
                        ("rq", MIX_W), ("rk", MIX_W), ("rv", MIX_W), ("rg", MIX_W),
                        ("gl", N_BRANCH * D_MODEL)):
        parts[name] = w_in[:, o:o + width]
        o += width
    hr = MLA_ROPE // 2
    kr1, kr2 = parts["kr"][:, :hr], parts["kr"][:, hr:]
    main = jnp.concatenate(
        [parts["cq"], parts["ckv"], kr1, kr1, kr2, kr2, parts["mq"], parts["mk"], parts["mv"],
         parts["mo"], parts["rq"], parts["rk"], parts["rv"], parts["rg"], parts["gl"]], axis=1)
    gates_t = jnp.concatenate([parts["mi"], parts["mi"], parts["mf"], parts["mf"]], axis=1).T
    return main.astype(BF16), gates_t.astype(BF16)


def _pack_w_uq(w_uq):
    per = MLA_NOPE + MLA_ROPE
    hr = MLA_ROPE // 2
    nope = [w_uq[:, h * per:h * per + MLA_NOPE] for h in range(N_HEADS)]
    x1 = [w_uq[:, h * per + MLA_NOPE:h * per + MLA_NOPE + hr] for h in range(N_HEADS)]
    x2 = [w_uq[:, h * per + MLA_NOPE + hr:(h + 1) * per] for h in range(N_HEADS)]
    pairs = [x1[0], x1[1], x2[0], x2[1], x1[2], x1[3], x2[2], x2[3]]
    return jnp.concatenate(nope + pairs, axis=1).astype(BF16)


def _pack_w_ukv(w_ukv):
    per = MLA_NOPE + HEAD_DIM
    keys = [w_ukv[:, h * per:h * per + MLA_NOPE] for h in range(N_HEADS)]
    vals = [w_ukv[:, h * per + MLA_NOPE:(h + 1) * per] for h in range(N_HEADS)]
    return jnp.concatenate(keys + vals, axis=1).astype(BF16)


def kernel(x, positions, ffn1_w_gate, ffn1_w_up, ffn1_w_down, ln1_g, ln1_b, w_in, mla_q_norm_g, mla_w_uq, mla_kv_norm_g, mla_w_ukv, mlstm_conv_w, mlstm_conv_b, mlstm_gate_b, mlstm_norm_g, ret_norm_g, w_br_mla, w_br_mlstm, w_br_ret, gate_b, w_out, ln2_g, ln2_b, ffn2_w_gate, ffn2_w_up, ffn2_w_down, ln3_g, ln3_b):
    b, s, d = x.shape
    depth = w_in.shape[0]
    alpha = (2 * depth) ** DEPTH_ALPHA_POW
    n = b * s
    tables = _rope_tables(positions, tm=1024)
    row = lambda a: a.reshape(1, -1)
    ff_split = (D_FF // MXU_WIDTH + 1) // 2 * MXU_WIDTH
    ffn = functools.partial(_ffn_ln, alpha=alpha, tm=1024, sub=512,
                            ff_chunks=((0, ff_split), (ff_split, D_FF)))

    h = x.reshape(n, d)
    for l in range(depth):
        h = ffn(h, ffn1_w_gate, ffn1_w_up, ffn1_w_down, row(ln1_g[l]), row(ln1_b[l]), layer=l)

        w_main, w_gates_t = _pack_w_in(w_in[l])
        gbias = mlstm_gate_b[l]
        gate_bias = jnp.concatenate([gbias[:N_HEADS], gbias[:N_HEADS], gbias[N_HEADS:], gbias[N_HEADS:]]
                                    ).reshape(2 * SUBLANES, 1)
        (q, k, v, mq, mk, mkt, mv, ig, fg, mo, rq, rk, rkt, rv, rg, gate) = _in_proj(
            h.reshape(b, s, d), w_main, w_gates_t, gate_bias, row(mla_q_norm_g[l]), _pack_w_uq(mla_w_uq[l]),
            row(mla_kv_norm_g[l]), _pack_w_ukv(mla_w_ukv[l]), mlstm_conv_w[l], row(mlstm_conv_b[l]),
            row(gate_b[l]), tables, tm=512)
        y_mla = _attention(q, k, v, tq=4096, tk=512)
        r, eloc, a_sc, b_sc, cols = _gate_prep(ig, fg)
        y_ml, y_rt = _recurrent(mq, mk, mkt, mv, r, eloc, a_sc, b_sc, cols, mo, rq, rk, rkt, rv, rg,
                                row(mlstm_norm_g[l]), row(ret_norm_g[l]))
        h = _merge(h, y_mla.reshape(n, MIX_W), y_ml.reshape(n, MIX_W), y_rt.reshape(n, MIX_W),
                   gate.reshape(n, N_BRANCH * D_MODEL), w_br_mla[l].astype(BF16), w_br_mlstm[l].astype(BF16),
                   w_br_ret[l].astype(BF16), w_out[l].astype(BF16), row(ln2_g[l]), row(ln2_b[l]),
                   alpha=alpha, tm=512)

        h = ffn(h, ffn2_w_gate, ffn2_w_up, ffn2_w_down, row(ln3_g[l]), row(ln3_b[l]), layer=l)
    return h.reshape(b, s, d)
```

```python
import functools
import math

import jax
import jax.numpy as jnp
from jax import lax
from jax.experimental import pallas as pl
from jax.experimental.pallas import tpu as pltpu

D_MODEL = 1024
D_FF = 2816
N_HEADS = 4
HEAD_DIM = 128
MIX_W = N_HEADS * HEAD_DIM
MLA_Q_LORA = 384
MLA_KV_LORA = 256
MLA_NOPE = 128
MLA_ROPE = 64
MLA_QK = 2 * HEAD_DIM
CONV_K = 4
N_BRANCH = 3
CHUNK = 128
ROPE_THETA = 10000.0
NORM_EPS = 1e-5
NEG_BIG = -1e30
DEPTH_ALPHA_POW = 0.25

LANES = 128
SUBLANES = 8
MXU_WIDTH = 256
VMEM_LIMIT_BYTES = 56 * 1024 * 1024

_C_CQ = 0
_C_CKV = _C_CQ + MLA_Q_LORA
_C_KR = _C_CKV + MLA_KV_LORA
_C_MQK = _C_KR + LANES
_C_MV = _C_MQK + 2 * MIX_W
_C_MO = _C_MV + MIX_W
_C_RQ = _C_MO + MIX_W
_C_RK = _C_RQ + MIX_W
_C_RV = _C_RK + MIX_W
_C_RG = _C_RV + MIX_W
_C_GL = _C_RG + MIX_W
_C_END = _C_GL + N_BRANCH * D_MODEL

BF16 = jnp.bfloat16
F32 = jnp.float32


def _resident(shape):
    nd = len(shape)
    return pl.BlockSpec(shape, lambda *_: (0,) * nd, pipeline_mode=pl.Buffered(1))


def _params(sem):
    return pltpu.CompilerParams(dimension_semantics=sem, vmem_limit_bytes=VMEM_LIMIT_BYTES)


def _layer_norm(y, g, b):
    mu = jnp.mean(y, axis=-1, keepdims=True)
    yc = y - mu
    var = jnp.mean(yc * yc, axis=-1, keepdims=True)
    return yc * lax.rsqrt(var + NORM_EPS) * g + b


def _rms_norm(y, g):
    return y * lax.rsqrt(jnp.mean(y * y, axis=-1, keepdims=True) + NORM_EPS) * g


def _sigmoid(x):
    return 1.0 / (1.0 + jnp.exp(-x))


def _dot(a, b):
    return jnp.dot(a, b, preferred_element_type=F32)


def _dot_nt(a, b):
    return lax.dot_general(a, b, (((1,), (1,)), ((), ())), preferred_element_type=F32)


FFN_LOAD_CHUNKS = 8


def _stage_weights_bf16(jobs, sem_ref):
    copies = []
    uses = {}
    for src, stage, sem0, dst in jobs:
        slot = uses.get(id(stage), 0) % 2
        uses[id(stage)] = uses.get(id(stage), 0) + 1
        copies.append((pltpu.make_async_copy(src, stage.at[slot], sem_ref.at[sem0 + slot]), stage, slot, dst))
    copies[0][0].start()
    for i, (copy, stage, slot, dst) in enumerate(copies):
        if i + 1 < len(copies):
            copies[i + 1][0].start()
        copy.wait()
        dst[...] = stage[slot].astype(BF16)


def _ffn_ln_kernel(x_ref, wg_hbm, wu_hbm, wd_hbm, g_ref, b_ref, o_ref,
                   wg_ref, wu_ref, wd_ref, stage_up_ref, stage_dn_ref, sem_ref, *, layer, alpha, sub, ff_chunks):
    @pl.when(pl.program_id(0) == 0)
    def _():
        up_rows = D_MODEL // FFN_LOAD_CHUNKS
        dn_rows = D_FF // FFN_LOAD_CHUNKS
        jobs = []
        for w_hbm, w_ref in ((wg_hbm, wg_ref), (wu_hbm, wu_ref)):
            for c in range(FFN_LOAD_CHUNKS):
                rows = pl.ds(c * up_rows, up_rows)
                jobs.append((w_hbm.at[layer, rows, :], stage_up_ref, 0, w_ref.at[rows, :]))
        for c in range(FFN_LOAD_CHUNKS):
            rows = pl.ds(c * dn_rows, dn_rows)
            jobs.append((wd_hbm.at[layer, rows, :], stage_dn_ref, 2, wd_ref.at[rows, :]))
        _stage_weights_bf16(jobs, sem_ref)

    for r0 in range(0, x_ref.shape[0], sub):
        rows = slice(r0, r0 + sub)
        x = x_ref[rows, :]
        xb = x.astype(BF16)
        acc = None
        for c0, c1 in ff_chunks:
            gate = _dot(xb, wg_ref[:, c0:c1])
            up = _dot(xb, wu_ref[:, c0:c1])
            act = (gate * _sigmoid(gate) * up).astype(BF16)
            part = _dot(act, wd_ref[c0:c1, :])
            acc = part if acc is None else acc + part
        o_ref[rows, :] = _layer_norm(alpha * x + 0.5 * acc, g_ref[...], b_ref[...])


def _ffn_ln(x, wg, wu, wd, g, b, *, layer, alpha, tm, sub, ff_chunks):
    n = x.shape[0]
    tm = min(tm, n)
    sub = min(sub, tm)
    row = pl.BlockSpec((tm, D_MODEL), lambda i: (i, 0))
    hbm = pl.BlockSpec(memory_space=pl.ANY)
    return pl.pallas_call(
        functools.partial(_ffn_ln_kernel, layer=layer, alpha=alpha, sub=sub, ff_chunks=ff_chunks),
        grid=(n // tm,),
        in_specs=[row, hbm, hbm, hbm, _resident((1, D_MODEL)), _resident((1, D_MODEL))],
        out_specs=row,
        out_shape=jax.ShapeDtypeStruct((n, D_MODEL), F32),
        scratch_shapes=[pltpu.VMEM((D_MODEL, D_FF), BF16), pltpu.VMEM((D_MODEL, D_FF), BF16),
                        pltpu.VMEM((D_FF, D_MODEL), BF16),
                        pltpu.VMEM((2, D_MODEL // FFN_LOAD_CHUNKS, D_FF), F32),
                        pltpu.VMEM((2, D_FF // FFN_LOAD_CHUNKS, D_MODEL), F32),
                        pltpu.SemaphoreType.DMA((4,))],
        compiler_params=_params(("arbitrary",)),
        name="ffn_ln",
    )(x, wg, wu, wd, g, b)


def _rope_table_kernel(pos_ref, freq_ref, sign_ref, cos_ret_ref, sin_ret_ref, cos_mla_ref, sin_mla_ref):
    pos = pos_ref[...].astype(F32)
    ang_ret = pos * freq_ref[0:1, :]
    ang_mla = pos * freq_ref[1:2, :]
    cos_ret_ref[...] = jnp.cos(ang_ret)
    sin_ret_ref[...] = jnp.sin(ang_ret) * sign_ref[...]
    cos_mla_ref[...] = jnp.cos(ang_mla)
    sin_mla_ref[...] = jnp.sin(ang_mla) * sign_ref[...]


def _rope_tables(positions, *, tm):
    n = positions.size
    tm = min(tm, n)
    half_ret = HEAD_DIM // 2
    half_mla = MLA_ROPE // 2
    f_ret = ROPE_THETA ** (-jnp.arange(half_ret, dtype=F32) / half_ret)
    f_mla = ROPE_THETA ** (-jnp.arange(half_mla, dtype=F32) / half_mla)
    freq = jnp.stack([jnp.tile(f_ret, 2), jnp.tile(f_mla, 4)])
    sign = jnp.concatenate([-jnp.ones((1, LANES // 2), F32), jnp.ones((1, LANES // 2), F32)], axis=1)
    tab = jax.ShapeDtypeStruct((n, LANES), F32)
    row = pl.BlockSpec((tm, LANES), lambda i: (i, 0))
    return pl.pallas_call(
        _rope_table_kernel,
        grid=(n // tm,),
        in_specs=[pl.BlockSpec((tm, 1), lambda i: (i, 0)), _resident((2, LANES)), _resident((1, LANES))],
        out_specs=[row, row, row, row],
        out_shape=[tab, tab, tab, tab],
        compiler_params=_params(("arbitrary",)),
        name="rope_tables",
    )(positions.reshape(n, 1), freq, sign)


def _in_proj_kernel(x_ref, w_ref, wgt_ref, gbias_ref, qg_ref, wuq_ref, kvg_ref, wukv_ref,
                    convw_ref, convb_ref, gateb_ref, cr_ref, sr_ref, cm_ref, sm_ref,
                    q_ref, k_ref, v_ref, mq_ref, mk_ref, mkt_ref, mv_ref, ig_ref, fg_ref, mo_ref,
                    rq_ref, rk_ref, rkt_ref, rv_ref, rg_ref, gate_ref, ext_ref, *, tm):
    xb = x_ref[0].astype(BF16)
    half = LANES // 2

    def proj(c0, c1):
        return _dot(xb, w_ref[:, c0:c1])

    def rope(t, cos, sin):
        return t * cos + pltpu.roll(t, half, axis=1) * sin

    cos_m, sin_m = cm_ref[...], sm_ref[...]
    cos_r, sin_r = cr_ref[...], sr_ref[...]

    cq = _rms_norm(proj(_C_CQ, _C_CKV), qg_ref[...]).astype(BF16)
    qh = _dot(cq, wuq_ref[...])
    sm_scale = (MLA_NOPE + MLA_ROPE) ** -0.5 * math.log2(math.e)
    pairs = [rope(qh[:, MIX_W + p * LANES:MIX_W + (p + 1) * LANES], cos_m, sin_m) * sm_scale
             for p in range(2)]
    for h in range(N_HEADS):
        q_ref[0, h, :, 0:LANES] = (qh[:, h * LANES:(h + 1) * LANES] * sm_scale).astype(BF16)
        q_ref[0, h, :, LANES:2 * LANES] = pairs[h // 2].astype(BF16)

    ckv = _rms_norm(proj(_C_CKV, _C_KR), kvg_ref[...]).astype(BF16)
    kv = _dot(ckv, wukv_ref[...])
    kr = rope(proj(_C_KR, _C_MQK), cos_m, sin_m)
    lane = lax.broadcasted_iota(jnp.int32, (tm, LANES), 1)
    group_odd = (lane // (MLA_ROPE // 2)) % 2 == 1
    kr_par = [jnp.where(group_odd, 0.0, kr).astype(BF16), jnp.where(group_odd, kr, 0.0).astype(BF16)]
    for h in range(N_HEADS):
        k_ref[0, h, :, 0:LANES] = kv[:, h * LANES:(h + 1) * LANES].astype(BF16)
        k_ref[0, h, :, LANES:2 * LANES] = kr_par[h % 2]
    v_ref[0] = kv[:, MIX_W:2 * MIX_W].astype(BF16)

    @pl.when(pl.program_id(1) == 0)
    def _():
        ext_ref[0:SUBLANES, :] = jnp.zeros((SUBLANES, 2 * MIX_W), F32)

    qk_pre = proj(_C_MQK, _C_MV)
    ext_ref[SUBLANES:SUBLANES + tm, :] = qk_pre
    conv = qk_pre * convw_ref[CONV_K - 1:CONV_K, :] + convb_ref[...]
    for lag in range(1, CONV_K):
        conv = conv + ext_ref[SUBLANES - lag:SUBLANES - lag + tm, :] * convw_ref[CONV_K - 1 - lag:CONV_K - lag, :]
    ext_ref[0:SUBLANES, :] = qk_pre[tm - SUBLANES:tm, :]
    conv = conv * _sigmoid(conv)
    mq_ref[0] = conv[:, 0:MIX_W].astype(BF16)
    mk = conv[:, MIX_W:2 * MIX_W] * HEAD_DIM ** -0.5
    mk_ref[0] = mk.astype(BF16)
    mkt_ref[0] = mk.T.astype(BF16)
    mv_ref[0] = proj(_C_MV, _C_MO).astype(BF16)
    mo_ref[0] = _sigmoid(proj(_C_MO, _C_RQ)).astype(BF16)

    gates_t = _dot_nt(wgt_ref[...], xb) + gbias_ref[...]
    for c in range(tm // CHUNK):
        ig_ref[0, c] = gates_t[0:SUBLANES, c * CHUNK:(c + 1) * CHUNK]
        fg_ref[0, c] = gates_t[SUBLANES:2 * SUBLANES, c * CHUNK:(c + 1) * CHUNK]

    rq = proj(_C_RQ, _C_RK)
    rk = proj(_C_RK, _C_RV)
    for h in range(N_HEADS):
        sl = slice(h * LANES, (h + 1) * LANES)
        rq_ref[0, :, sl] = rope(rq[:, sl], cos_r, sin_r).astype(BF16)
        rk_h = rope(rk[:, sl], cos_r, sin_r) * HEAD_DIM ** -0.5
        rk_ref[0, :, sl] = rk_h.astype(BF16)
        rkt_ref[0, sl, :] = rk_h.T.astype(BF16)
    rv_ref[0] = proj(_C_RV, _C_RG).astype(BF16)
    rg = proj(_C_RG, _C_GL)
    rg_ref[0] = (rg * _sigmoid(rg)).astype(BF16)

    for br in range(N_BRANCH):
        c0 = _C_GL + br * D_MODEL
        sl = slice(br * D_MODEL, (br + 1) * D_MODEL)
        gate_ref[0, :, sl] = _sigmoid(proj(c0, c0 + D_MODEL) + gateb_ref[:, sl]).astype(BF16)


def _in_proj(x, w_main, w_gates_t, gate_bias, q_norm_g, w_uq, kv_norm_g, w_ukv, conv_w, conv_b,
             gate_b, tables, *, tm):
    b, s, _ = x.shape
    tm = min(tm, s)
    nt = s // tm
    cos_r, sin_r, cos_m, sin_m = tables

    def tok(width):
        return pl.BlockSpec((1, tm, width), lambda i, j: (i, j, 0))

    head4 = pl.BlockSpec((1, N_HEADS, tm, MLA_QK), lambda i, j: (i, 0, j, 0))
    gate_t = pl.BlockSpec((1, tm // CHUNK, SUBLANES, CHUNK), lambda i, j: (i, j, 0, 0))
    tok_t = pl.BlockSpec((1, MIX_W, tm), lambda i, j: (i, 0, j))
    table = pl.BlockSpec((tm, LANES), lambda i, j: (i * nt + j, 0))
    act = lambda width: jax.ShapeDtypeStruct((b, s, width), BF16)
    qk4 = jax.ShapeDtypeStruct((b, N_HEADS, s, MLA_QK), BF16)
    gt = jax.ShapeDtypeStruct((b, s // CHUNK, SUBLANES, CHUNK), F32)
    act_t = jax.ShapeDtypeStruct((b, MIX_W, s), BF16)
    return pl.pallas_call(
        functools.partial(_in_proj_kernel, tm=tm),
        grid=(b, nt),
        in_specs=[tok(D_MODEL), _resident(w_main.shape), _resident(w_gates_t.shape),
                  _resident(gate_bias.shape), _resident(q_norm_g.shape), _resident(w_uq.shape),
                  _resident(kv_norm_g.shape), _resident(w_ukv.shape), _resident(conv_w.shape),
                  _resident(conv_b.shape), _resident(gate_b.shape), table, table, table, table],
        out_specs=[head4, head4, tok(MIX_W), tok(MIX_W), tok(MIX_W), tok_t, tok(MIX_W), gate_t, gate_t,
                   tok(MIX_W), tok(MIX_W), tok(MIX_W), tok_t, tok(MIX_W), tok(MIX_W),
                   tok(N_BRANCH * D_MODEL)],
        out_shape=[qk4, qk4, act(MIX_W), act(MIX_W), act(MIX_W), act_t, act(MIX_W), gt, gt,
                   act(MIX_W), act(MIX_W), act(MIX_W), act_t, act(MIX_W), act(MIX_W),
                   act(N_BRANCH * D_MODEL)],
        scratch_shapes=[pltpu.VMEM((tm + SUBLANES, 2 * MIX_W), F32)],
        compiler_params=_params(("arbitrary", "arbitrary")),
        name="in_proj",
    )(x, w_main, w_gates_t, gate_bias, q_norm_g, w_uq, kv_norm_g, w_ukv, conv_w, conv_b, gate_b,
      cos_r, sin_r, cos_m, sin_m)


def _attn_kernel(q_ref, k_ref, v_ref, o_ref, m_ref, acc_ref, *, tq, tk):
    i = pl.program_id(2)
    n_chain = tq // tk
    m_ref[...] = jnp.full(m_ref.shape, NEG_BIG, F32)
    acc_ref[...] = jnp.zeros(acc_ref.shape, F32)
    ones = jnp.ones((tk, HEAD_DIM), BF16)

    def chain_step(c, start, diagonal):
        rows = slice(c * tk, (c + 1) * tk)
        k = k_ref[0, 0, pl.ds(start, tk), :]
        v1 = jnp.concatenate([v_ref[0, pl.ds(start, tk), :], ones], axis=1)
        s = _dot_nt(q_ref[0, 0, rows, :], k)
        if diagonal:
            row = lax.broadcasted_iota(jnp.int32, (tk, tk), 0)
            col = lax.broadcasted_iota(jnp.int32, (tk, tk), 1)
            s = jnp.where(col <= row, s, NEG_BIG)
        blocks = [s[:, j * LANES:(j + 1) * LANES] for j in range(tk // LANES)]
        m_prev = m_ref[rows, :]
        m_blk = functools.reduce(jnp.maximum, blocks)
        m_new = jnp.maximum(m_prev, jnp.max(m_blk, axis=-1, keepdims=True))
        alpha = jnp.exp2(m_prev - m_new)
        p = jnp.concatenate([jnp.exp2(blk - m_new) for blk in blocks], axis=1).astype(BF16)
        acc_ref[rows, :] = jnp.concatenate([alpha, alpha], axis=1) * acc_ref[rows, :] + _dot(p, v1)
        m_ref[rows, :] = m_new

    def body(j, carry):
        start = pl.multiple_of(j * tk, tk)
        for c in range(n_chain):
            chain_step(c, start, False)
        return carry

    lax.fori_loop(0, i * n_chain, body, 0)
    for c in range(n_chain):
        for jj in range(c + 1):
            chain_step(c, pl.multiple_of(i * tq + jj * tk, tk), jj == c)
    acc = acc_ref[...]
    o_ref[0] = (acc[:, 0:HEAD_DIM] / acc[:, HEAD_DIM:2 * HEAD_DIM]).astype(o_ref.dtype)


def _attention(q, k, v, *, tq, tk):
    b, h, s, _ = q.shape
    tq, tk = min(tq, s), min(tk, s)
    return pl.pallas_call(
        functools.partial(_attn_kernel, tq=tq, tk=tk),
        grid=(b, h, s // tq),
        in_specs=[pl.BlockSpec((1, 1, tq, MLA_QK), lambda bi, hi, i: (bi, hi, i, 0)),
                  pl.BlockSpec((1, 1, s, MLA_QK), lambda bi, hi, i: (bi, hi, 0, 0)),
                  pl.BlockSpec((1, s, HEAD_DIM), lambda bi, hi, i: (bi, 0, hi))],
        out_specs=pl.BlockSpec((1, tq, HEAD_DIM), lambda bi, hi, i: (bi, i, hi)),
        out_shape=jax.ShapeDtypeStruct((b, s, MIX_W), BF16),
        scratch_shapes=[pltpu.VMEM((tq, LANES), F32), pltpu.VMEM((tq, 2 * HEAD_DIM), F32)],
        compiler_params=_params(("arbitrary", "arbitrary", "arbitrary")),
        name="mla_attention",
    )(q, k, v)


def _lane_scan(x, combine, identity):
    lane = lax.broadcasted_iota(jnp.int32, x.shape, 1)
    shift = 1
    while shift < LANES:
        x = combine(x, jnp.where(lane >= shift, pltpu.roll(x, shift, axis=1), identity))
        shift *= 2
    return x


def _gate_prep_kernel(ig_ref, fg_ref, r_ref, eloc_ref, a_ref, b_ref, cols_ref, mprev_ref, stack_ref):
    nc = ig_ref.shape[1]
    rows = nc * SUBLANES
    L = CHUNK
    ig = ig_ref[0].reshape(rows, L)
    fg = fg_ref[0].reshape(rows, L)
    lf = jnp.minimum(fg, 0.0) - jnp.log(1.0 + jnp.exp(-jnp.abs(fg)))
    g = _lane_scan(lf, jnp.add, 0.0)
    g_last = jnp.broadcast_to(g[:, L - 1:L], (rows, L))
    r = ig - g
    w_loc = g_last + r
    m_loc = jnp.broadcast_to(jnp.max(w_loc, axis=-1, keepdims=True), (rows, L))
    eloc_ref[0] = jnp.exp(w_loc - m_loc).reshape(nc, SUBLANES, L)
    r_ref[0] = r.reshape(nc, SUBLANES, L)
    cmax = _lane_scan(r, jnp.maximum, NEG_BIG)

    m = jnp.zeros((SUBLANES, L), F32)
    for c in range(nc):
        sl = slice(c * SUBLANES, (c + 1) * SUBLANES)
        mprev_ref[sl, :] = m
        m = jnp.maximum(g_last[sl, :] + m, m_loc[sl, :])
    m_prev = mprev_ref[...]
    mx = jnp.maximum(m_prev, cmax)
    m_new = jnp.maximum(g_last + m_prev, m_loc)
    a_ref[0] = jnp.exp(g_last + m_prev - m_new).reshape(nc, SUBLANES, L)
    b_ref[0] = jnp.exp(m_loc - m_new).reshape(nc, SUBLANES, L)

    stack_ref[...] = jnp.zeros(stack_ref.shape, F32)
    stack_ref[:, 0:SUBLANES, :] = mx.reshape(nc, SUBLANES, L)
    stack_ref[:, SUBLANES:2 * SUBLANES, :] = jnp.exp(m_prev - mx).reshape(nc, SUBLANES, L)
    stack_ref[:, 2 * SUBLANES:3 * SUBLANES, :] = jnp.exp(-g - mx).reshape(nc, SUBLANES, L)

    def body(c, carry):
        cols_ref[0, c] = stack_ref[c].T
        return carry

    lax.fori_loop(0, nc, body, 0)


def _gate_prep(ig, fg):
    b, nc, _, _ = ig.shape
    rows = pl.BlockSpec((1, nc, SUBLANES, CHUNK), lambda i: (i, 0, 0, 0))
    cols = pl.BlockSpec((1, nc, CHUNK, LANES), lambda i: (i, 0, 0, 0))
    row_shape = jax.ShapeDtypeStruct((b, nc, SUBLANES, CHUNK), F32)
    return pl.pallas_call(
        _gate_prep_kernel,
        grid=(b,),
        in_specs=[rows, rows],
        out_specs=[rows, rows, rows, rows, cols],
        out_shape=[row_shape, row_shape, row_shape, row_shape,
                   jax.ShapeDtypeStruct((b, nc, CHUNK, LANES), F32)],
        scratch_shapes=[pltpu.VMEM((nc * SUBLANES, CHUNK), F32), pltpu.VMEM((nc, CHUNK, LANES), F32)],
        compiler_params=_params(("arbitrary",)),
        name="mlstm_gate_prep",
    )(ig, fg)


def _head_norm_gate(hh, g_row, gate):
    mu = jnp.mean(hh, axis=-1, keepdims=True)
    hc = hh - mu
    var = jnp.mean(hc * hc, axis=-1, keepdims=True)
    return hc * lax.rsqrt(var + NORM_EPS) * g_row * gate


def _recurrent_kernel(mq_ref, mk_ref, mkt_ref, mv_ref, r_ref, eloc_ref, a_ref, b_ref, cols_ref, mo_ref,
                      rq_ref, rk_ref, rkt_ref, rv_ref, rg_ref, mg_ref, rgn_ref, yml_ref, yrt_ref,
                      c_ref, r_state_ref, dec_ref):
    L = CHUNK
    row = lax.broadcasted_iota(jnp.int32, (L, L), 0)
    col = lax.broadcasted_iota(jnp.int32, (L, L), 1)
    causal = col <= row

    @pl.when(pl.program_id(0) == 0)
    def _():
        c_ref[...] = jnp.zeros(c_ref.shape, F32)
        r_state_ref[...] = jnp.zeros(r_state_ref.shape, F32)
        rowf = row.astype(F32)
        colf = col.astype(F32)
        for h in range(N_HEADS):
            lg = math.log(1.0 - 2.0 ** (-5.0 - h))
            dec_ref[h, 0] = jnp.where(causal, jnp.exp(jnp.where(causal, rowf - colf, 0.0) * lg), 0.0)
            dec_ref[h, 1] = jnp.exp((rowf + 1.0) * lg)
            dec_ref[h, 2] = jnp.exp((L - 1.0 - colf) * lg)

    ones = jnp.ones((L, HEAD_DIM), BF16)
    chains = [(bi, h) for bi in range(mq_ref.shape[0]) for h in range(N_HEADS)]
    head = lambda h: slice(h * HEAD_DIM, (h + 1) * HEAD_DIM)
    cols = [cols_ref[bi, 0] for bi in range(mq_ref.shape[0])]

    ml_scores = [_dot_nt(mq_ref[bi, :, head(h)], mk_ref[bi, :, head(h)]) for bi, h in chains]
    rt_scores = [_dot_nt(rq_ref[bi, :, head(h)], rk_ref[bi, :, head(h)]) for bi, h in chains]
    ml_inter = [_dot(mq_ref[bi, :, head(h)], c_ref[bi * N_HEADS + h].astype(BF16)) for bi, h in chains]
    rt_inter = [_dot(rq_ref[bi, :, head(h)], r_state_ref[bi * N_HEADS + h].astype(BF16)) for bi, h in chains]
    v1s = [jnp.concatenate([mv_ref[bi, :, head(h)], ones], axis=1) for bi, h in chains]

    ml_qk = []
    for (bi, h), s in zip(chains, ml_scores):
        p = jnp.exp(jnp.where(causal, r_ref[bi, 0][h:h + 1, :] - cols[bi][:, h:h + 1], NEG_BIG))
        ml_qk.append((s * p).astype(BF16))
    rt_qk = [(s * dec_ref[h, 0]).astype(BF16) for (bi, h), s in zip(chains, rt_scores)]

    ml_intra = [_dot(qk, v1) for qk, v1 in zip(ml_qk, v1s)]
    rt_intra = [_dot(qk, rv_ref[bi, :, head(h)]) for (bi, h), qk in zip(chains, rt_qk)]
    ml_kwt = [(mkt_ref[bi, head(h), :].astype(F32) * eloc_ref[bi, 0][h:h + 1, :]).astype(BF16)
              for bi, h in chains]
    rt_kwt = [(rkt_ref[bi, head(h), :].astype(F32) * dec_ref[h, 2]).astype(BF16) for bi, h in chains]
    ml_upd = [_dot(kwt, v1) for kwt, v1 in zip(ml_kwt, v1s)]
    rt_upd = [_dot(kwt, rv_ref[bi, :, head(h)]) for (bi, h), kwt in zip(chains, rt_kwt)]

    for i, (bi, h) in enumerate(chains):
        st = bi * N_HEADS + h
        e_inter_c = cols[bi][:, SUBLANES + h:SUBLANES + h + 1]
        e_floor_c = cols[bi][:, 2 * SUBLANES + h:2 * SUBLANES + h + 1]
        nd = ml_intra[i] + e_inter_c * ml_inter[i]
        hh = nd[:, 0:HEAD_DIM] / jnp.maximum(jnp.abs(nd[:, HEAD_DIM:2 * HEAD_DIM]), e_floor_c)
        yml_ref[bi, :, head(h)] = _head_norm_gate(
            hh, mg_ref[:, head(h)], mo_ref[bi, :, head(h)].astype(F32)).astype(BF16)
        yrt_ref[bi, :, head(h)] = _head_norm_gate(
            rt_intra[i] + dec_ref[h, 1] * rt_inter[i], rgn_ref[:, head(h)],
            rg_ref[bi, :, head(h)].astype(F32)).astype(BF16)
        c_ref[st] = a_ref[bi, 0][h:h + 1, 0:1] * c_ref[st] + b_ref[bi, 0][h:h + 1, 0:1] * ml_upd[i]
        g_chunk = math.exp(L * math.log(1.0 - 2.0 ** (-5.0 - h)))
        r_state_ref[st] = g_chunk * r_state_ref[st] + rt_upd[i]


def _recurrent(mq, mk, mkt, mv, r, eloc, a, bsc, cols, mo, rq, rk, rkt, rv, rg, mlstm_norm_g, ret_norm_g):
    b, s, _ = mq.shape
    tok = pl.BlockSpec((b, CHUNK, MIX_W), lambda c: (0, c, 0))
    tok_t = pl.BlockSpec((b, MIX_W, CHUNK), lambda c: (0, 0, c))
    rows = pl.BlockSpec((b, 1, SUBLANES, CHUNK), lambda c: (0, c, 0, 0))
    colsp = pl.BlockSpec((b, 1, CHUNK, LANES), lambda c: (0, c, 0, 0))
    out = jax.ShapeDtypeStruct((b, s, MIX_W), BF16)
    return pl.pallas_call(
        _recurrent_kernel,
        grid=(s // CHUNK,),
        in_specs=[tok, tok, tok_t, tok, rows, rows, rows, rows, colsp, tok,
                  tok, tok, tok_t, tok, tok, _resident((1, MIX_W)), _resident((1, MIX_W))],
        out_specs=[tok, tok],
        out_shape=[out, out],
        scratch_shapes=[pltpu.VMEM((b * N_HEADS, HEAD_DIM, 2 * HEAD_DIM), F32),
                        pltpu.VMEM((b * N_HEADS, HEAD_DIM, HEAD_DIM), F32),
                        pltpu.VMEM((N_HEADS, 3, CHUNK, CHUNK), F32)],
        compiler_params=_params(("arbitrary",)),
        name="recurrent_mixers",
    )(mq, mk, mkt, mv, r, eloc, a, bsc, cols, mo, rq, rk, rkt, rv, rg, mlstm_norm_g, ret_norm_g)


def _merge_kernel(x_ref, ya_ref, yb_ref, yc_ref, gate_ref, wa_ref, wb_ref, wc_ref, wo_ref, g_ref, b_ref,
                  o_ref, *, alpha, sub):
    for r0 in range(0, x_ref.shape[0], sub):
        rows = slice(r0, r0 + sub)
        merged = None
        for br, (y_ref, w_ref) in enumerate(((ya_ref, wa_ref), (yb_ref, wb_ref), (yc_ref, wc_ref))):
            gate = gate_ref[rows, br * D_MODEL:(br + 1) * D_MODEL].astype(F32)
            term = gate * _dot(y_ref[rows, :], w_ref[...])
            merged = term if merged is None else merged + term
        mix = _dot(merged.astype(BF16), wo_ref[...])
        o_ref[rows, :] = _layer_norm(alpha * x_ref[rows, :] + mix, g_ref[...], b_ref[...])


def _merge(x, y_mla, y_ml, y_rt, gate, w_a, w_b, w_c, w_out, g, b, *, alpha, tm, sub):
    n = x.shape[0]
    tm = min(tm, n)
    sub = min(sub, tm)
    tok = lambda width: pl.BlockSpec((tm, width), lambda i: (i, 0))
    return pl.pallas_call(
        functools.partial(_merge_kernel, alpha=alpha, sub=sub),
        grid=(n // tm,),
        in_specs=[tok(D_MODEL), tok(MIX_W), tok(MIX_W), tok(MIX_W), tok(N_BRANCH * D_MODEL),
                  _resident((MIX_W, D_MODEL)), _resident((MIX_W, D_MODEL)), _resident((MIX_W, D_MODEL)),
                  _resident((D_MODEL, D_MODEL)), _resident((1, D_MODEL)), _resident((1, D_MODEL))],
        out_specs=tok(D_MODEL),
        out_shape=jax.ShapeDtypeStruct((n, D_MODEL), F32),
        compiler_params=_params(("arbitrary",)),
        name="merge_out_ln",
    )(x, y_mla, y_ml, y_rt, gate, w_a, w_b, w_c, w_out, g, b)


def _pack_w_in(w_in):
    o = 0
    parts = {}
    for name, width in (("cq", MLA_Q_LORA), ("ckv", MLA_KV_LORA), ("kr", MLA_ROPE), ("mq", MIX_W),
                        ("mk", MIX_W), ("mv", MIX_W), ("mi", N_HEADS), ("mf", N_HEADS), ("mo", MIX_W),
                        ("rq", MIX_W), ("rk", MIX_W), ("rv", MIX_W), ("rg", MIX_W),
                        ("gl", N_BRANCH * D_MODEL)):
        parts[name] = w_in[:, o:o + width]
        o += width
    hr = MLA_ROPE // 2
    kr1, kr2 = parts["kr"][:, :hr], parts["kr"][:, hr:]
    main = jnp.concatenate(
        [parts["cq"], parts["ckv"], kr1, kr1, kr2, kr2, parts["mq"], parts["mk"], parts["mv"],
         parts["mo"], parts["rq"], parts["rk"], parts["rv"], parts["rg"], parts["gl"]], axis=1)
    gates_t = jnp.concatenate([parts["mi"], parts["mi"], parts["mf"], parts["mf"]], axis=1).T
    return main.astype(BF16), gates_t.astype(BF16)


def _pack_w_uq(w_uq):
    per = MLA_NOPE + MLA_ROPE
    hr = MLA_ROPE // 2
    nope = [w_uq[:, h * per:h * per + MLA_NOPE] for h in range(N_HEADS)]
    x1 = [w_uq[:, h * per + MLA_NOPE:h * per + MLA_NOPE + hr] for h in range(N_HEADS)]
    x2 = [w_uq[:, h * per + MLA_NOPE + hr:(h + 1) * per] for h in range(N_HEADS)]
    pairs = [x1[0], x1[1], x2[0], x2[1], x1[2], x1[3], x2[2], x2[3]]
    return jnp.concatenate(nope + pairs, axis=1).astype(BF16)


def _pack_w_ukv(w_ukv):
    per = MLA_NOPE + HEAD_DIM
    keys = [w_ukv[:, h * per:h * per + MLA_NOPE] for h in range(N_HEADS)]
    vals = [w_ukv[:, h * per + MLA_NOPE:(h + 1) * per] for h in range(N_HEADS)]
    return jnp.concatenate(keys + vals, axis=1).astype(BF16)


def kernel(x, positions, ffn1_w_gate, ffn1_w_up, ffn1_w_down, ln1_g, ln1_b, w_in, mla_q_norm_g, mla_w_uq, mla_kv_norm_g, mla_w_ukv, mlstm_conv_w, mlstm_conv_b, mlstm_gate_b, mlstm_norm_g, ret_norm_g, w_br_mla, w_br_mlstm, w_br_ret, gate_b, w_out, ln2_g, ln2_b, ffn2_w_gate, ffn2_w_up, ffn2_w_down, ln3_g, ln3_b):
    b, s, d = x.shape
    depth = w_in.shape[0]
    alpha = (2 * depth) ** DEPTH_ALPHA_POW
    n = b * s
    tables = _rope_tables(positions, tm=1024)
    row = lambda a: a.reshape(1, -1)
    ff_split = (D_FF // MXU_WIDTH + 1) // 2 * MXU_WIDTH
    ffn = functools.partial(_ffn_ln, alpha=alpha, tm=1024, sub=512,
                            ff_chunks=((0, ff_split), (ff_split, D_FF)))

    h = x.reshape(n, d)
    for l in range(depth):
        h = ffn(h, ffn1_w_gate, ffn1_w_up, ffn1_w_down, row(ln1_g[l]), row(ln1_b[l]), layer=l)

        w_main, w_gates_t = _pack_w_in(w_in[l])
        gbias = mlstm_gate_b[l]
        gate_bias = jnp.concatenate([gbias[:N_HEADS], gbias[:N_HEADS], gbias[N_HEADS:], gbias[N_HEADS:]]
                                    ).reshape(2 * SUBLANES, 1)
        (q, k, v, mq, mk, mkt, mv, ig, fg, mo, rq, rk, rkt, rv, rg, gate) = _in_proj(
            h.reshape(b, s, d), w_main, w_gates_t, gate_bias, row(mla_q_norm_g[l]), _pack_w_uq(mla_w_uq[l]),
            row(mla_kv_norm_g[l]), _pack_w_ukv(mla_w_ukv[l]), mlstm_conv_w[l], row(mlstm_conv_b[l]),
            row(gate_b[l]), tables, tm=512)
        y_mla = _attention(q, k, v, tq=4096, tk=512)
        r, eloc, a_sc, b_sc, cols = _gate_prep(ig, fg)
        y_ml, y_rt = _recurrent(mq, mk, mkt, mv, r, eloc, a_sc, b_sc, cols, mo, rq, rk, rkt, rv, rg,
                                row(mlstm_norm_g[l]), row(ret_norm_g[l]))
        h = _merge(h, y_mla.reshape(n, MIX_W), y_ml.reshape(n, MIX_W), y_rt.reshape(n, MIX_W),
                   gate.reshape(n, N_BRANCH * D_MODEL), w_br_mla[l].astype(BF16), w_br_mlstm[l].astype(BF16),
                   w_br_ret[l].astype(BF16), w_out[l].astype(BF16), row(ln2_g[l]), row(ln2_b[l]),
                   alpha=alpha, tm=1024, sub=512)

        h = ffn(h, ffn2_w_gate, ffn2_w_up, ffn2_w_down, row(ln3_g[l]), row(ln3_b[l]), layer=l)
    return h.reshape(b, s, d)
```

```python
import functools
import math

import jax
import jax.numpy as jnp
from jax import lax
from jax.experimental import pallas as pl
from jax.experimental.pallas import tpu as pltpu

D_MODEL = 1024
D_FF = 2816
N_HEADS = 4
HEAD_DIM = 128
MIX_W = N_HEADS * HEAD_DIM
MLA_Q_LORA = 384
MLA_KV_LORA = 256
MLA_NOPE = 128
MLA_ROPE = 64
MLA_QK = 2 * HEAD_DIM
CONV_K = 4
N_BRANCH = 3
CHUNK = 128
ROPE_THETA = 10000.0
NORM_EPS = 1e-5
NEG_BIG = -1e30
DEPTH_ALPHA_POW = 0.25

LANES = 128
SUBLANES = 8
MXU_WIDTH = 256
VMEM_LIMIT_BYTES = 56 * 1024 * 1024

_C_CQ = 0
_C_CKV = _C_CQ + MLA_Q_LORA
_C_KR = _C_CKV + MLA_KV_LORA
_C_MQK = _C_KR + LANES
_C_MV = _C_MQK + 2 * MIX_W
_C_MO = _C_MV + MIX_W
_C_RQ = _C_MO + MIX_W
_C_RK = _C_RQ + MIX_W
_C_RV = _C_RK + MIX_W
_C_RG = _C_RV + MIX_W
_C_GL = _C_RG + MIX_W
_C_END = _C_GL + N_BRANCH * D_MODEL

BF16 = jnp.bfloat16
F32 = jnp.float32


def _resident(shape):
    nd = len(shape)
    return pl.BlockSpec(shape, lambda *_: (0,) * nd, pipeline_mode=pl.Buffered(1))


def _params(sem):
    return pltpu.CompilerParams(dimension_semantics=sem, vmem_limit_bytes=VMEM_LIMIT_BYTES)


def _layer_norm(y, g, b):
    mu = jnp.mean(y, axis=-1, keepdims=True)
    yc = y - mu
    var = jnp.mean(yc * yc, axis=-1, keepdims=True)
    return yc * lax.rsqrt(var + NORM_EPS) * g + b


def _rms_norm(y, g):
    return y * lax.rsqrt(jnp.mean(y * y, axis=-1, keepdims=True) + NORM_EPS) * g


def _sigmoid(x):
    return 1.0 / (1.0 + jnp.exp(-x))


def _dot(a, b):
    return jnp.dot(a, b, preferred_element_type=F32)


def _dot_nt(a, b):
    return lax.dot_general(a, b, (((1,), (1,)), ((), ())), preferred_element_type=F32)


FFN_LOAD_CHUNKS = 8


def _stage_weights_bf16(jobs, sem_ref):
    copies = []
    uses = {}
    for src, stage, sem0, dst in jobs:
        slot = uses.get(id(stage), 0) % 2
        uses[id(stage)] = uses.get(id(stage), 0) + 1
        copies.append((pltpu.make_async_copy(src, stage.at[slot], sem_ref.at[sem0 + slot]), stage, slot, dst))
    copies[0][0].start()
    for i, (copy, stage, slot, dst) in enumerate(copies):
        if i + 1 < len(copies):
            copies[i + 1][0].start()
        copy.wait()
        dst[...] = stage[slot].astype(BF16)


def _ffn_ln_kernel(x_ref, wg_hbm, wu_hbm, wd_hbm, g_ref, b_ref, o_ref,
                   wg_ref, wu_ref, wd_ref, stage_up_ref, stage_dn_ref, sem_ref, *, layer, alpha, sub, ff_chunks):
    @pl.when(pl.program_id(0) == 0)
    def _():
        up_rows = D_MODEL // FFN_LOAD_CHUNKS
        dn_rows = D_FF // FFN_LOAD_CHUNKS
        jobs = []
        for w_hbm, w_ref in ((wg_hbm, wg_ref), (wu_hbm, wu_ref)):
            for c in range(FFN_LOAD_CHUNKS):
                rows = pl.ds(c * up_rows, up_rows)
                jobs.append((w_hbm.at[layer, rows, :], stage_up_ref, 0, w_ref.at[rows, :]))
        for c in range(FFN_LOAD_CHUNKS):
            rows = pl.ds(c * dn_rows, dn_rows)
            jobs.append((wd_hbm.at[layer, rows, :], stage_dn_ref, 2, wd_ref.at[rows, :]))
        _stage_weights_bf16(jobs, sem_ref)

    for r0 in range(0, x_ref.shape[0], sub):
        rows = slice(r0, r0 + sub)
        x = x_ref[rows, :]
        xb = x.astype(BF16)
        acc = None
        for c0, c1 in ff_chunks:
            gate = _dot(xb, wg_ref[:, c0:c1])
            up = _dot(xb, wu_ref[:, c0:c1])
            act = (gate * _sigmoid(gate) * up).astype(BF16)
            part = _dot(act, wd_ref[c0:c1, :])
            acc = part if acc is None else acc + part
        o_ref[rows, :] = _layer_norm(alpha * x + 0.5 * acc, g_ref[...], b_ref[...])


def _ffn_ln(x, wg, wu, wd, g, b, *, layer, alpha, tm, sub, ff_chunks):
    n = x.shape[0]
    tm = min(tm, n)
    sub = min(sub, tm)
    row = pl.BlockSpec((tm, D_MODEL), lambda i: (i, 0))
    hbm = pl.BlockSpec(memory_space=pl.ANY)
    return pl.pallas_call(
        functools.partial(_ffn_ln_kernel, layer=layer, alpha=alpha, sub=sub, ff_chunks=ff_chunks),
        grid=(n // tm,),
        in_specs=[row, hbm, hbm, hbm, _resident((1, D_MODEL)), _resident((1, D_MODEL))],
        out_specs=row,
        out_shape=jax.ShapeDtypeStruct((n, D_MODEL), F32),
        scratch_shapes=[pltpu.VMEM((D_MODEL, D_FF), BF16), pltpu.VMEM((D_MODEL, D_FF), BF16),
                        pltpu.VMEM((D_FF, D_MODEL), BF16),
                        pltpu.VMEM((2, D_MODEL // FFN_LOAD_CHUNKS, D_FF), F32),
                        pltpu.VMEM((2, D_FF // FFN_LOAD_CHUNKS, D_MODEL), F32),
                        pltpu.SemaphoreType.DMA((4,))],
        compiler_params=_params(("arbitrary",)),
        name="ffn_ln",
    )(x, wg, wu, wd, g, b)


def _rope_table_kernel(pos_ref, freq_ref, sign_ref, cos_ret_ref, sin_ret_ref, cos_mla_ref, sin_mla_ref):
    ang = pos_ref[...].astype(F32) * freq_ref[...]
    lane = lax.broadcasted_iota(jnp.int32, ang.shape, 1)
    half, quarter = LANES // 2, LANES // 4

    def tables(t):
        ret = jnp.where(lane < half, t, pltpu.roll(t, half, axis=1))
        mla = pltpu.roll(t, half, axis=1)
        mla = jnp.where(lane < quarter, mla, pltpu.roll(mla, quarter, axis=1))
        return ret, jnp.where(lane < half, mla, pltpu.roll(mla, half, axis=1))

    cos_ret, cos_mla = tables(jnp.cos(ang))
    sin_ret, sin_mla = tables(jnp.sin(ang))
    cos_ret_ref[...] = cos_ret
    sin_ret_ref[...] = sin_ret * sign_ref[...]
    cos_mla_ref[...] = cos_mla
    sin_mla_ref[...] = sin_mla * sign_ref[...]


def _rope_tables(positions, *, tm):
    n = positions.size
    tm = min(tm, n)
    half_ret = HEAD_DIM // 2
    half_mla = MLA_ROPE // 2
    f_ret = ROPE_THETA ** (-jnp.arange(half_ret, dtype=F32) / half_ret)
    f_mla = ROPE_THETA ** (-jnp.arange(half_mla, dtype=F32) / half_mla)
    freq = jnp.concatenate([f_ret, f_mla, jnp.zeros((LANES - half_ret - half_mla,), F32)]).reshape(1, LANES)
    sign = jnp.concatenate([-jnp.ones((1, LANES // 2), F32), jnp.ones((1, LANES // 2), F32)], axis=1)
    tab = jax.ShapeDtypeStruct((n, LANES), F32)
    row = pl.BlockSpec((tm, LANES), lambda i: (i, 0))
    return pl.pallas_call(
        _rope_table_kernel,
        grid=(n // tm,),
        in_specs=[pl.BlockSpec((tm, 1), lambda i: (i, 0)), _resident((1, LANES)), _resident((1, LANES))],
        out_specs=[row, row, row, row],
        out_shape=[tab, tab, tab, tab],
        compiler_params=_params(("arbitrary",)),
        name="rope_tables",
    )(positions.reshape(n, 1), freq, sign)


def _in_proj_kernel(x_ref, w_ref, wgt_ref, gbias_ref, qg_ref, wuq_ref, kvg_ref, wukv_ref,
                    convw_ref, convb_ref, gateb_ref, cr_ref, sr_ref, cm_ref, sm_ref,
                    q_ref, k_ref, v_ref, mq_ref, mk_ref, mkt_ref, mv_ref, ig_ref, fg_ref, mo_ref,
                    rq_ref, rk_ref, rkt_ref, rv_ref, rg_ref, gate_ref, ext_ref, *, tm):
    xb = x_ref[0].astype(BF16)
    half = LANES // 2

    def proj(c0, c1):
        return _dot(xb, w_ref[:, c0:c1])

    def rope(t, cos, sin):
        return t * cos + pltpu.roll(t, half, axis=1) * sin

    cos_m, sin_m = cm_ref[...], sm_ref[...]
    cos_r, sin_r = cr_ref[...], sr_ref[...]

    cq = _rms_norm(proj(_C_CQ, _C_CKV), qg_ref[...]).astype(BF16)
    ckv = _rms_norm(proj(_C_CKV, _C_KR), kvg_ref[...]).astype(BF16)
    qh = _dot(cq, wuq_ref[...])
    kv = _dot(ckv, wukv_ref[...])
    sm_scale = (MLA_NOPE + MLA_ROPE) ** -0.5 * math.log2(math.e)
    pairs = [rope(qh[:, MIX_W + p * LANES:MIX_W + (p + 1) * LANES], cos_m, sin_m) * sm_scale
             for p in range(2)]
    for h in range(N_HEADS):
        q_ref[0, h, :, 0:LANES] = (qh[:, h * LANES:(h + 1) * LANES] * sm_scale).astype(BF16)
        q_ref[0, h, :, LANES:2 * LANES] = pairs[h // 2].astype(BF16)

    kr = rope(proj(_C_KR, _C_MQK), cos_m, sin_m)
    lane = lax.broadcasted_iota(jnp.int32, (tm, LANES), 1)
    group_odd = (lane // (MLA_ROPE // 2)) % 2 == 1
    kr_par = [jnp.where(group_odd, 0.0, kr).astype(BF16), jnp.where(group_odd, kr, 0.0).astype(BF16)]
    for h in range(N_HEADS):
        k_ref[0, h, :, 0:LANES] = kv[:, h * LANES:(h + 1) * LANES].astype(BF16)
        k_ref[0, h, :, LANES:2 * LANES] = kr_par[h % 2]
    v_ref[0] = kv[:, MIX_W:2 * MIX_W].astype(BF16)

    @pl.when(pl.program_id(1) == 0)
    def _():
        ext_ref[0:SUBLANES, :] = jnp.zeros((SUBLANES, 2 * MIX_W), F32)

    qk_pre = proj(_C_MQK, _C_MV)
    mv_z = proj(_C_MV, _C_MO)
    mo_z = proj(_C_MO, _C_RQ)
    ext_ref[SUBLANES:SUBLANES + tm, :] = qk_pre
    conv = qk_pre * convw_ref[CONV_K - 1:CONV_K, :] + convb_ref[...]
    for lag in range(1, CONV_K):
        conv = conv + ext_ref[SUBLANES - lag:SUBLANES - lag + tm, :] * convw_ref[CONV_K - 1 - lag:CONV_K - lag, :]
    ext_ref[0:SUBLANES, :] = qk_pre[tm - SUBLANES:tm, :]
    conv = conv * _sigmoid(conv)
    mq_ref[0] = conv[:, 0:MIX_W].astype(BF16)
    mk = conv[:, MIX_W:2 * MIX_W] * HEAD_DIM ** -0.5
    mk_ref[0] = mk.astype(BF16)
    mkt_ref[0] = mk.T.astype(BF16)
    mv_ref[0] = mv_z.astype(BF16)
    mo_ref[0] = _sigmoid(mo_z).astype(BF16)

    gates_t = _dot_nt(wgt_ref[...], xb) + gbias_ref[...]
    for c in range(tm // CHUNK):
        ig_ref[0, c] = gates_t[0:SUBLANES, c * CHUNK:(c + 1) * CHUNK]
        fg_ref[0, c] = gates_t[SUBLANES:2 * SUBLANES, c * CHUNK:(c + 1) * CHUNK]

    rq = proj(_C_RQ, _C_RK)
    rk = proj(_C_RK, _C_RV)
    for h in range(N_HEADS):
        sl = slice(h * LANES, (h + 1) * LANES)
        rq_ref[0, :, sl] = rope(rq[:, sl], cos_r, sin_r).astype(BF16)
        rk_h = rope(rk[:, sl], cos_r, sin_r) * HEAD_DIM ** -0.5
        rk_ref[0, :, sl] = rk_h.astype(BF16)
        rkt_ref[0, sl, :] = rk_h.T.astype(BF16)
    rv_ref[0] = proj(_C_RV, _C_RG).astype(BF16)
    rg = proj(_C_RG, _C_GL)
    rg_ref[0] = (rg * _sigmoid(rg)).astype(BF16)

    for br in range(N_BRANCH):
        c0 = _C_GL + br * D_MODEL
        sl = slice(br * D_MODEL, (br + 1) * D_MODEL)
        gate_ref[0, :, sl] = _sigmoid(proj(c0, c0 + D_MODEL) + gateb_ref[:, sl]).astype(BF16)


def _in_proj(x, w_main, w_gates_t, gate_bias, q_norm_g, w_uq, kv_norm_g, w_ukv, conv_w, conv_b,
             gate_b, tables, *, tm):
    b, s, _ = x.shape
    tm = min(tm, s)
    nt = s // tm
    cos_r, sin_r, cos_m, sin_m = tables

    def tok(width):
        return pl.BlockSpec((1, tm, width), lambda i, j: (i, j, 0))

    head4 = pl.BlockSpec((1, N_HEADS, tm, MLA_QK), lambda i, j: (i, 0, j, 0))
    gate_t = pl.BlockSpec((1, tm // CHUNK, SUBLANES, CHUNK), lambda i, j: (i, j, 0, 0))
    tok_t = pl.BlockSpec((1, MIX_W, tm), lambda i, j: (i, 0, j))
    table = pl.BlockSpec((tm, LANES), lambda i, j: (i * nt + j, 0))
    act = lambda width: jax.ShapeDtypeStruct((b, s, width), BF16)
    qk4 = jax.ShapeDtypeStruct((b, N_HEADS, s, MLA_QK), BF16)
    gt = jax.ShapeDtypeStruct((b, s // CHUNK, SUBLANES, CHUNK), F32)
    act_t = jax.ShapeDtypeStruct((b, MIX_W, s), BF16)
    return pl.pallas_call(
        functools.partial(_in_proj_kernel, tm=tm),
        grid=(b, nt),
        in_specs=[tok(D_MODEL), _resident(w_main.shape), _resident(w_gates_t.shape),
                  _resident(gate_bias.shape), _resident(q_norm_g.shape), _resident(w_uq.shape),
                  _resident(kv_norm_g.shape), _resident(w_ukv.shape), _resident(conv_w.shape),
                  _resident(conv_b.shape), _resident(gate_b.shape), table, table, table, table],
        out_specs=[head4, head4, tok(MIX_W), tok(MIX_W), tok(MIX_W), tok_t, tok(MIX_W), gate_t, gate_t,
                   tok(MIX_W), tok(MIX_W), tok(MIX_W), tok_t, tok(MIX_W), tok(MIX_W),
                   tok(N_BRANCH * D_MODEL)],
        out_shape=[qk4, qk4, act(MIX_W), act(MIX_W), act(MIX_W), act_t, act(MIX_W), gt, gt,
                   act(MIX_W), act(MIX_W), act(MIX_W), act_t, act(MIX_W), act(MIX_W),
                   act(N_BRANCH * D_MODEL)],
        scratch_shapes=[pltpu.VMEM((tm + SUBLANES, 2 * MIX_W), F32)],
        compiler_params=_params(("arbitrary", "arbitrary")),
        name="in_proj",
    )(x, w_main, w_gates_t, gate_bias, q_norm_g, w_uq, kv_norm_g, w_ukv, conv_w, conv_b, gate_b,
      cos_r, sin_r, cos_m, sin_m)


def _attn_kernel(q_ref, k_ref, v_ref, o_ref, m_ref, acc_ref, *, tq, tk):
    i = pl.program_id(2)
    n_chain = tq // tk
    m_ref[...] = jnp.full(m_ref.shape, NEG_BIG, F32)
    acc_ref[...] = jnp.zeros(acc_ref.shape, F32)
    ones = jnp.ones((tk, HEAD_DIM), BF16)

    def chain_step(c, start, diagonal):
        rows = slice(c * tk, (c + 1) * tk)
        k = k_ref[0, 0, pl.ds(start, tk), :]
        v1 = jnp.concatenate([v_ref[0, pl.ds(start, tk), :], ones], axis=1)
        s = _dot_nt(q_ref[0, 0, rows, :], k)
        if diagonal:
            row = lax.broadcasted_iota(jnp.int32, (tk, tk), 0)
            col = lax.broadcasted_iota(jnp.int32, (tk, tk), 1)
            s = jnp.where(col <= row, s, NEG_BIG)
        blocks = [s[:, j * LANES:(j + 1) * LANES] for j in range(tk // LANES)]
        m_prev = m_ref[rows, :]
        m_blk = functools.reduce(jnp.maximum, blocks)
        m_new = jnp.maximum(m_prev, jnp.max(m_blk, axis=-1, keepdims=True))
        alpha = jnp.exp2(m_prev - m_new)
        p = jnp.concatenate([jnp.exp2(blk - m_new) for blk in blocks], axis=1).astype(BF16)
        acc_ref[rows, :] = jnp.concatenate([alpha, alpha], axis=1) * acc_ref[rows, :] + _dot(p, v1)
        m_ref[rows, :] = m_new

    def body(j, carry):
        start = pl.multiple_of(j * tk, tk)
        for c in range(n_chain):
            chain_step(c, start, False)
        return carry

    lax.fori_loop(0, i * n_chain, body, 0)
    for c in range(n_chain):
        for jj in range(c + 1):
            chain_step(c, pl.multiple_of(i * tq + jj * tk, tk), jj == c)
    acc = acc_ref[...]
    o_ref[0] = (acc[:, 0:HEAD_DIM] / acc[:, HEAD_DIM:2 * HEAD_DIM]).astype(o_ref.dtype)


def _attention(q, k, v, *, tq, tk):
    b, h, s, _ = q.shape
    tq, tk = min(tq, s), min(tk, s)
    return pl.pallas_call(
        functools.partial(_attn_kernel, tq=tq, tk=tk),
        grid=(b, h, s // tq),
        in_specs=[pl.BlockSpec((1, 1, tq, MLA_QK), lambda bi, hi, i: (bi, hi, i, 0)),
                  pl.BlockSpec((1, 1, s, MLA_QK), lambda bi, hi, i: (bi, hi, 0, 0)),
                  pl.BlockSpec((1, s, HEAD_DIM), lambda bi, hi, i: (bi, 0, hi))],
        out_specs=pl.BlockSpec((1, tq, HEAD_DIM), lambda bi, hi, i: (bi, i, hi)),
        out_shape=jax.ShapeDtypeStruct((b, s, MIX_W), BF16),
        scratch_shapes=[pltpu.VMEM((tq, LANES), F32), pltpu.VMEM((tq, 2 * HEAD_DIM), F32)],
        compiler_params=_params(("arbitrary", "arbitrary", "arbitrary")),
        name="mla_attention",
    )(q, k, v)


def _lane_scan(x, combine, identity):
    lane = lax.broadcasted_iota(jnp.int32, x.shape, 1)
    shift = 1
    while shift < LANES:
        x = combine(x, jnp.where(lane >= shift, pltpu.roll(x, shift, axis=1), identity))
        shift *= 2
    return x


def _gate_prep_kernel(ig_ref, fg_ref, r_ref, eloc_ref, a_ref, b_ref, cols_ref, mprev_ref, stack_ref):
    nc = ig_ref.shape[1]
    rows = nc * SUBLANES
    L = CHUNK
    ig = ig_ref[0].reshape(rows, L)
    fg = fg_ref[0].reshape(rows, L)
    lf = jnp.minimum(fg, 0.0) - jnp.log(1.0 + jnp.exp(-jnp.abs(fg)))
    g = _lane_scan(lf, jnp.add, 0.0)
    g_last = jnp.broadcast_to(g[:, L - 1:L], (rows, L))
    r = ig - g
    w_loc = g_last + r
    m_loc = jnp.broadcast_to(jnp.max(w_loc, axis=-1, keepdims=True), (rows, L))
    eloc_ref[0] = jnp.exp(w_loc - m_loc).reshape(nc, SUBLANES, L)
    r_ref[0] = r.reshape(nc, SUBLANES, L)
    cmax = _lane_scan(r, jnp.maximum, NEG_BIG)

    m = jnp.zeros((SUBLANES, L), F32)
    for c in range(nc):
        sl = slice(c * SUBLANES, (c + 1) * SUBLANES)
        mprev_ref[sl, :] = m
        m = jnp.maximum(g_last[sl, :] + m, m_loc[sl, :])
    m_prev = mprev_ref[...]
    mx = jnp.maximum(m_prev, cmax)
    m_new = jnp.maximum(g_last + m_prev, m_loc)
    a_ref[0] = jnp.exp(g_last + m_prev - m_new).reshape(nc, SUBLANES, L)
    b_ref[0] = jnp.exp(m_loc - m_new).reshape(nc, SUBLANES, L)

    stack_ref[...] = jnp.zeros(stack_ref.shape, F32)
    stack_ref[:, 0:SUBLANES, :] = mx.reshape(nc, SUBLANES, L)
    stack_ref[:, SUBLANES:2 * SUBLANES, :] = jnp.exp(m_prev - mx).reshape(nc, SUBLANES, L)
    stack_ref[:, 2 * SUBLANES:3 * SUBLANES, :] = jnp.exp(-g - mx).reshape(nc, SUBLANES, L)

    def body(c, carry):
        cols_ref[0, c] = stack_ref[c].T
        return carry

    lax.fori_loop(0, nc, body, 0)


def _gate_prep(ig, fg):
    b, nc, _, _ = ig.shape
    rows = pl.BlockSpec((1, nc, SUBLANES, CHUNK), lambda i: (i, 0, 0, 0))
    cols = pl.BlockSpec((1, nc, CHUNK, LANES), lambda i: (i, 0, 0, 0))
    row_shape = jax.ShapeDtypeStruct((b, nc, SUBLANES, CHUNK), F32)
    return pl.pallas_call(
        _gate_prep_kernel,
        grid=(b,),
        in_specs=[rows, rows],
        out_specs=[rows, rows, rows, rows, cols],
        out_shape=[row_shape, row_shape, row_shape, row_shape,
                   jax.ShapeDtypeStruct((b, nc, CHUNK, LANES), F32)],
        scratch_shapes=[pltpu.VMEM((nc * SUBLANES, CHUNK), F32), pltpu.VMEM((nc, CHUNK, LANES), F32)],
        compiler_params=_params(("arbitrary",)),
        name="mlstm_gate_prep",
    )(ig, fg)


def _head_norm_gate(hh, g_row, gate):
    mu = jnp.mean(hh, axis=-1, keepdims=True)
    hc = hh - mu
    var = jnp.mean(hc * hc, axis=-1, keepdims=True)
    return hc * lax.rsqrt(var + NORM_EPS) * g_row * gate


def _recurrent_kernel(mq_ref, mk_ref, mkt_ref, mv_ref, r_ref, eloc_ref, a_ref, b_ref, cols_ref, mo_ref,
                      rq_ref, rk_ref, rkt_ref, rv_ref, rg_ref, mg_ref, rgn_ref, yml_ref, yrt_ref,
                      c_ref, r_state_ref, dec_ref):
    L = CHUNK
    row = lax.broadcasted_iota(jnp.int32, (L, L), 0)
    col = lax.broadcasted_iota(jnp.int32, (L, L), 1)
    causal = col <= row

    @pl.when(pl.program_id(0) == 0)
    def _():
        c_ref[...] = jnp.zeros(c_ref.shape, F32)
        r_state_ref[...] = jnp.zeros(r_state_ref.shape, F32)
        rowf = row.astype(F32)
        colf = col.astype(F32)
        for h in range(N_HEADS):
            lg = math.log(1.0 - 2.0 ** (-5.0 - h))
            dec_ref[h, 0] = jnp.where(causal, jnp.exp(jnp.where(causal, rowf - colf, 0.0) * lg), 0.0)
            dec_ref[h, 1] = jnp.exp((rowf + 1.0) * lg)
            dec_ref[h, 2] = jnp.exp((L - 1.0 - colf) * lg)

    ones = jnp.ones((L, HEAD_DIM), BF16)
    chains = [(bi, h) for bi in range(mq_ref.shape[0]) for h in range(N_HEADS)]
    head = lambda h: slice(h * HEAD_DIM, (h + 1) * HEAD_DIM)
    cols = [cols_ref[bi, 0] for bi in range(mq_ref.shape[0])]

    ml_scores = [_dot_nt(mq_ref[bi, :, head(h)], mk_ref[bi, :, head(h)]) for bi, h in chains]
    rt_scores = [_dot_nt(rq_ref[bi, :, head(h)], rk_ref[bi, :, head(h)]) for bi, h in chains]
    ml_inter = [_dot(mq_ref[bi, :, head(h)], c_ref[bi * N_HEADS + h].astype(BF16)) for bi, h in chains]
    rt_inter = [_dot(rq_ref[bi, :, head(h)], r_state_ref[bi * N_HEADS + h].astype(BF16)) for bi, h in chains]
    v1s = [jnp.concatenate([mv_ref[bi, :, head(h)], ones], axis=1) for bi, h in chains]

    ml_qk = []
    for (bi, h), s in zip(chains, ml_scores):
        p = jnp.exp(jnp.where(causal, r_ref[bi, 0][h:h + 1, :] - cols[bi][:, h:h + 1], NEG_BIG))
        ml_qk.append((s * p).astype(BF16))
    rt_qk = [(s * dec_ref[h, 0]).astype(BF16) for (bi, h), s in zip(chains, rt_scores)]

    ml_intra = [_dot(qk, v1) for qk, v1 in zip(ml_qk, v1s)]
    rt_intra = [_dot(qk, rv_ref[bi, :, head(h)]) for (bi, h), qk in zip(chains, rt_qk)]
    ml_kwt = [(mkt_ref[bi, head(h), :].astype(F32) * eloc_ref[bi, 0][h:h + 1, :]).astype(BF16)
              for bi, h in chains]
    rt_kwt = [(rkt_ref[bi, head(h), :].astype(F32) * dec_ref[h, 2]).astype(BF16) for bi, h in chains]
    ml_upd = [_dot(kwt, v1) for kwt, v1 in zip(ml_kwt, v1s)]
    rt_upd = [_dot(kwt, rv_ref[bi, :, head(h)]) for (bi, h), kwt in zip(chains, rt_kwt)]

    for i, (bi, h) in enumerate(chains):
        st = bi * N_HEADS + h
        e_inter_c = cols[bi][:, SUBLANES + h:SUBLANES + h + 1]
        e_floor_c = cols[bi][:, 2 * SUBLANES + h:2 * SUBLANES + h + 1]
        nd = ml_intra[i] + e_inter_c * ml_inter[i]
        hh = nd[:, 0:HEAD_DIM] / jnp.maximum(jnp.abs(nd[:, HEAD_DIM:2 * HEAD_DIM]), e_floor_c)
        yml_ref[bi, :, head(h)] = _head_norm_gate(
            hh, mg_ref[:, head(h)], mo_ref[bi, :, head(h)].astype(F32)).astype(BF16)
        yrt_ref[bi, :, head(h)] = _head_norm_gate(
            rt_intra[i] + dec_ref[h, 1] * rt_inter[i], rgn_ref[:, head(h)],
            rg_ref[bi, :, head(h)].astype(F32)).astype(BF16)
        c_ref[st] = a_ref[bi, 0][h:h + 1, 0:1] * c_ref[st] + b_ref[bi, 0][h:h + 1, 0:1] * ml_upd[i]
        g_chunk = math.exp(L * math.log(1.0 - 2.0 ** (-5.0 - h)))
        r_state_ref[st] = g_chunk * r_state_ref[st] + rt_upd[i]


def _recurrent(mq, mk, mkt, mv, r, eloc, a, bsc, cols, mo, rq, rk, rkt, rv, rg, mlstm_norm_g, ret_norm_g):
    b, s, _ = mq.shape
    tok = pl.BlockSpec((b, CHUNK, MIX_W), lambda c: (0, c, 0))
    tok_t = pl.BlockSpec((b, MIX_W, CHUNK), lambda c: (0, 0, c))
    rows = pl.BlockSpec((b, 1, SUBLANES, CHUNK), lambda c: (0, c, 0, 0))
    colsp = pl.BlockSpec((b, 1, CHUNK, LANES), lambda c: (0, c, 0, 0))
    out = jax.ShapeDtypeStruct((b, s, MIX_W), BF16)
    return pl.pallas_call(
        _recurrent_kernel,
        grid=(s // CHUNK,),
        in_specs=[tok, tok, tok_t, tok, rows, rows, rows, rows, colsp, tok,
                  tok, tok, tok_t, tok, tok, _resident((1, MIX_W)), _resident((1, MIX_W))],
        out_specs=[tok, tok],
        out_shape=[out, out],
        scratch_shapes=[pltpu.VMEM((b * N_HEADS, HEAD_DIM, 2 * HEAD_DIM), F32),
                        pltpu.VMEM((b * N_HEADS, HEAD_DIM, HEAD_DIM), F32),
                        pltpu.VMEM((N_HEADS, 3, CHUNK, CHUNK), F32)],
        compiler_params=_params(("arbitrary",)),
        name="recurrent_mixers",
    )(mq, mk, mkt, mv, r, eloc, a, bsc, cols, mo, rq, rk, rkt, rv, rg, mlstm_norm_g, ret_norm_g)


def _merge_kernel(x_ref, ya_ref, yb_ref, yc_ref, gate_ref, wa_ref, wb_ref, wc_ref, wo_ref, g_ref, b_ref,
                  o_ref, *, alpha, sub):
    for r0 in range(0, x_ref.shape[0], sub):
        rows = slice(r0, r0 + sub)
        merged = None
        for br, (y_ref, w_ref) in enumerate(((ya_ref, wa_ref), (yb_ref, wb_ref), (yc_ref, wc_ref))):
            gate = gate_ref[rows, br * D_MODEL:(br + 1) * D_MODEL].astype(F32)
            term = gate * _dot(y_ref[rows, :], w_ref[...])
            merged = term if merged is None else merged + term
        mix = _dot(merged.astype(BF16), wo_ref[...])
        o_ref[rows, :] = _layer_norm(alpha * x_ref[rows, :] + mix, g_ref[...], b_ref[...])


def _merge(x, y_mla, y_ml, y_rt, gate, w_a, w_b, w_c, w_out, g, b, *, alpha, tm, sub):
    n = x.shape[0]
    tm = min(tm, n)
    sub = min(sub, tm)
    tok = lambda width: pl.BlockSpec((tm, width), lambda i: (i, 0))
    return pl.pallas_call(
        functools.partial(_merge_kernel, alpha=alpha, sub=sub),
        grid=(n // tm,),
        in_specs=[tok(D_MODEL), tok(MIX_W), tok(MIX_W), tok(MIX_W), tok(N_BRANCH * D_MODEL),
                  _resident((MIX_W, D_MODEL)), _resident((MIX_W, D_MODEL)), _resident((MIX_W, D_MODEL)),
                  _resident((D_MODEL, D_MODEL)), _resident((1, D_MODEL)), _resident((1, D_MODEL))],
        out_specs=tok(D_MODEL),
        out_shape=jax.ShapeDtypeStruct((n, D_MODEL), F32),
        compiler_params=_params(("arbitrary",)),
        name="merge_out_ln",
    )(x, y_mla, y_ml, y_rt, gate, w_a, w_b, w_c, w_out, g, b)


def _pack_w_in(w_in):
    o = 0
    parts = {}
    for name, width in (("cq", MLA_Q_LORA), ("ckv", MLA_KV_LORA), ("kr", MLA_ROPE), ("mq", MIX_W),
                        ("mk", MIX_W), ("mv", MIX_W), ("mi", N_HEADS), ("mf", N_HEADS), ("mo", MIX_W),
                        ("rq", MIX_W), ("rk", MIX_W), ("rv", MIX_W), ("rg", MIX_W),
                        ("gl", N_BRANCH * D_MODEL)):
        parts[name] = w_in[:, o:o + width]
        o += width
    hr = MLA_ROPE // 2
    kr1, kr2 = parts["kr"][:, :hr], parts["kr"][:, hr:]
    main = jnp.concatenate(
        [parts["cq"], parts["ckv"], kr1, kr1, kr2, kr2, parts["mq"], parts["mk"], parts["mv"],
         parts["mo"], parts["rq"], parts["rk"], parts["rv"], parts["rg"], parts["gl"]], axis=1)
    gates_t = jnp.concatenate([parts["mi"], parts["mi"], parts["mf"], parts["mf"]], axis=1).T
    return main.astype(BF16), gates_t.astype(BF16)


def _pack_w_uq(w_uq):
    per = MLA_NOPE + MLA_ROPE
    hr = MLA_ROPE // 2
    nope = [w_uq[:, h * per:h * per + MLA_NOPE] for h in range(N_HEADS)]
    x1 = [w_uq[:, h * per + MLA_NOPE:h * per + MLA_NOPE + hr] for h in range(N_HEADS)]
    x2 = [w_uq[:, h * per + MLA_NOPE + hr:(h + 1) * per] for h in range(N_HEADS)]
    pairs = [x1[0], x1[1], x2[0], x2[1], x1[2], x1[3], x2[2], x2[3]]
    return jnp.concatenate(nope + pairs, axis=1).astype(BF16)


def _pack_w_ukv(w_ukv):
    per = MLA_NOPE + HEAD_DIM
    keys = [w_ukv[:, h * per:h * per + MLA_NOPE] for h in range(N_HEADS)]
    vals = [w_ukv[:, h * per + MLA_NOPE:(h + 1) * per] for h in range(N_HEADS)]
    return jnp.concatenate(keys + vals, axis=1).astype(BF16)


def kernel(x, positions, ffn1_w_gate, ffn1_w_up, ffn1_w_down, ln1_g, ln1_b, w_in, mla_q_norm_g, mla_w_uq, mla_kv_norm_g, mla_w_ukv, mlstm_conv_w, mlstm_conv_b, mlstm_gate_b, mlstm_norm_g, ret_norm_g, w_br_mla, w_br_mlstm, w_br_ret, gate_b, w_out, ln2_g, ln2_b, ffn2_w_gate, ffn2_w_up, ffn2_w_down, ln3_g, ln3_b):
    b, s, d = x.shape
    depth = w_in.shape[0]
    alpha = (2 * depth) ** DEPTH_ALPHA_POW
    n = b * s
    tables = _rope_tables(positions, tm=1024)
    row = lambda a: a.reshape(1, -1)
    ff_split = (D_FF // MXU_WIDTH + 1) // 2 * MXU_WIDTH
    ffn = functools.partial(_ffn_ln, alpha=alpha, tm=1024, sub=512,
                            ff_chunks=((0, ff_split), (ff_split, D_FF)))

    h = x.reshape(n, d)
    for l in range(depth):
        h = ffn(h, ffn1_w_gate, ffn1_w_up, ffn1_w_down, row(ln1_g[l]), row(ln1_b[l]), layer=l)

        w_main, w_gates_t = _pack_w_in(w_in[l])
        gbias = mlstm_gate_b[l]
        gate_bias = jnp.concatenate([gbias[:N_HEADS], gbias[:N_HEADS], gbias[N_HEADS:], gbias[N_HEADS:]]
                                    ).reshape(2 * SUBLANES, 1)
        (q, k, v, mq, mk, mkt, mv, ig, fg, mo, rq, rk, rkt, rv, rg, gate) = _in_proj(
            h.reshape(b, s, d), w_main, w_gates_t, gate_bias, row(mla_q_norm_g[l]), _pack_w_uq(mla_w_uq[l]),
            row(mla_kv_norm_g[l]), _pack_w_ukv(mla_w_ukv[l]), mlstm_conv_w[l], row(mlstm_conv_b[l]),
            row(gate_b[l]), tables, tm=512)
        y_mla = _attention(q, k, v, tq=4096, tk=512)
        r, eloc, a_sc, b_sc, cols = _gate_prep(ig, fg)
        y_ml, y_rt = _recurrent(mq, mk, mkt, mv, r, eloc, a_sc, b_sc, cols, mo, rq, rk, rkt, rv, rg,
                                row(mlstm_norm_g[l]), row(ret_norm_g[l]))
        h = _merge(h, y_mla.reshape(n, MIX_W), y_ml.reshape(n, MIX_W), y_rt.reshape(n, MIX_W),
                   gate.reshape(n, N_BRANCH * D_MODEL), w_br_mla[l].astype(BF16), w_br_mlstm[l].astype(BF16),
                   w_br_ret[l].astype(BF16), w_out[l].astype(BF16), row(ln2_g[l]), row(ln2_b[l]),
                   alpha=alpha, tm=1024, sub=512)

        h = ffn(h, ffn2_w_gate, ffn2_w_up, ffn2_w_down, row(ln3_g[l]), row(ln3_b[l]), layer=l)
    return h.reshape(b, s, d)
```

```python
import functools
import math

import jax
import jax.numpy as jnp
from jax import lax
from jax.experimental import pallas as pl
from jax.experimental.pallas import tpu as pltpu

D_MODEL = 1024
D_FF = 2816
N_HEADS = 4
HEAD_DIM = 128
MIX_W = N_HEADS * HEAD_DIM
MLA_Q_LORA = 384
MLA_KV_LORA = 256
MLA_NOPE = 128
MLA_ROPE = 64
MLA_QK = 2 * HEAD_DIM
CONV_K = 4
N_BRANCH = 3
CHUNK = 128
ROPE_THETA = 10000.0
NORM_EPS = 1e-5
NEG_BIG = -1e30
DEPTH_ALPHA_POW = 0.25

LANES = 128
SUBLANES = 8
MXU_WIDTH = 256
VMEM_LIMIT_BYTES = 56 * 1024 * 1024

ROW_BLOCK = 2 * MXU_WIDTH
TOKEN_TILE = 2 * ROW_BLOCK
IN_PROJ_TILE = ROW_BLOCK
ATTN_K_TILE = ROW_BLOCK
ATTN_Q_TILE = 8 * ATTN_K_TILE
ROPE_TILE = 1024

_C_CQ = 0
_C_CKV = _C_CQ + MLA_Q_LORA
_C_KR = _C_CKV + MLA_KV_LORA
_C_MQK = _C_KR + LANES
_C_MV = _C_MQK + 2 * MIX_W
_C_MO = _C_MV + MIX_W
_C_RQ = _C_MO + MIX_W
_C_RK = _C_RQ + MIX_W
_C_RV = _C_RK + MIX_W
_C_RG = _C_RV + MIX_W
_C_GL = _C_RG + MIX_W
_C_END = _C_GL + N_BRANCH * D_MODEL

BF16 = jnp.bfloat16
F32 = jnp.float32


def _resident(shape):
    nd = len(shape)
    return pl.BlockSpec(shape, lambda *_: (0,) * nd, pipeline_mode=pl.Buffered(1))


def _params(sem):
    return pltpu.CompilerParams(dimension_semantics=sem, vmem_limit_bytes=VMEM_LIMIT_BYTES)


def _layer_norm(y, g, b):
    mu = jnp.mean(y, axis=-1, keepdims=True)
    yc = y - mu
    var = jnp.mean(yc * yc, axis=-1, keepdims=True)
    return yc * lax.rsqrt(var + NORM_EPS) * g + b


def _rms_norm(y, g):
    return y * lax.rsqrt(jnp.mean(y * y, axis=-1, keepdims=True) + NORM_EPS) * g


def _sigmoid(x):
    return 1.0 / (1.0 + jnp.exp(-x))


def _dot(a, b):
    return jnp.dot(a, b, preferred_element_type=F32)


def _dot_nt(a, b):
    return lax.dot_general(a, b, (((1,), (1,)), ((), ())), preferred_element_type=F32)


FFN_LOAD_CHUNKS = 8


def _stage_weights_bf16(jobs, sem_ref):
    copies = []
    uses = {}
    for src, stage, sem0, dst in jobs:
        slot = uses.get(id(stage), 0) % 2
        uses[id(stage)] = uses.get(id(stage), 0) + 1
        copies.append((pltpu.make_async_copy(src, stage.at[slot], sem_ref.at[sem0 + slot]), stage, slot, dst))
    copies[0][0].start()
    for i, (copy, stage, slot, dst) in enumerate(copies):
        if i + 1 < len(copies):
            copies[i + 1][0].start()
        copy.wait()
        dst[...] = stage[slot].astype(BF16)


def _ffn_ln_kernel(x_ref, wg_hbm, wu_hbm, wd_hbm, g_ref, b_ref, o_ref,
                   wg_ref, wu_ref, wd_ref, stage_up_ref, stage_dn_ref, sem_ref, *, layer, alpha, sub, ff_chunks):
    @pl.when(pl.program_id(0) == 0)
    def _():
        up_rows = D_MODEL // FFN_LOAD_CHUNKS
        dn_rows = D_FF // FFN_LOAD_CHUNKS
        jobs = []
        for w_hbm, w_ref in ((wg_hbm, wg_ref), (wu_hbm, wu_ref)):
            for c in range(FFN_LOAD_CHUNKS):
                rows = pl.ds(c * up_rows, up_rows)
                jobs.append((w_hbm.at[layer, rows, :], stage_up_ref, 0, w_ref.at[rows, :]))
        for c in range(FFN_LOAD_CHUNKS):
            rows = pl.ds(c * dn_rows, dn_rows)
            jobs.append((wd_hbm.at[layer, rows, :], stage_dn_ref, 2, wd_ref.at[rows, :]))
        _stage_weights_bf16(jobs, sem_ref)

    for r0 in range(0, x_ref.shape[0], sub):
        rows = slice(r0, r0 + sub)
        x = x_ref[rows, :]
        xb = x.astype(BF16)
        acc = None
        for c0, c1 in ff_chunks:
            gate = _dot(xb, wg_ref[:, c0:c1])
            up = _dot(xb, wu_ref[:, c0:c1])
            act = (gate * _sigmoid(gate) * up).astype(BF16)
            part = _dot(act, wd_ref[c0:c1, :])
            acc = part if acc is None else acc + part
        o_ref[rows, :] = _layer_norm(alpha * x + 0.5 * acc, g_ref[...], b_ref[...])


def _ffn_ln(x, wg, wu, wd, g, b, *, layer, alpha, tm, sub, ff_chunks):
    n = x.shape[0]
    tm = min(tm, n)
    sub = min(sub, tm)
    row = pl.BlockSpec((tm, D_MODEL), lambda i: (i, 0))
    hbm = pl.BlockSpec(memory_space=pl.ANY)
    return pl.pallas_call(
        functools.partial(_ffn_ln_kernel, layer=layer, alpha=alpha, sub=sub, ff_chunks=ff_chunks),
        grid=(n // tm,),
        in_specs=[row, hbm, hbm, hbm, _resident((1, D_MODEL)), _resident((1, D_MODEL))],
        out_specs=row,
        out_shape=jax.ShapeDtypeStruct((n, D_MODEL), F32),
        scratch_shapes=[pltpu.VMEM((D_MODEL, D_FF), BF16), pltpu.VMEM((D_MODEL, D_FF), BF16),
                        pltpu.VMEM((D_FF, D_MODEL), BF16),
                        pltpu.VMEM((2, D_MODEL // FFN_LOAD_CHUNKS, D_FF), F32),
                        pltpu.VMEM((2, D_FF // FFN_LOAD_CHUNKS, D_MODEL), F32),
                        pltpu.SemaphoreType.DMA((4,))],
        compiler_params=_params(("arbitrary",)),
        name="ffn_ln",
    )(x, wg, wu, wd, g, b)


def _rope_table_kernel(pos_ref, freq_ref, sign_ref, cos_ret_ref, sin_ret_ref, cos_mla_ref, sin_mla_ref):
    ang = pos_ref[...].astype(F32) * freq_ref[...]
    lane = lax.broadcasted_iota(jnp.int32, ang.shape, 1)
    half, quarter = LANES // 2, LANES // 4

    def tables(t):
        ret = jnp.where(lane < half, t, pltpu.roll(t, half, axis=1))
        mla = pltpu.roll(t, half, axis=1)
        mla = jnp.where(lane < quarter, mla, pltpu.roll(mla, quarter, axis=1))
        return ret, jnp.where(lane < half, mla, pltpu.roll(mla, half, axis=1))

    cos_ret, cos_mla = tables(jnp.cos(ang))
    sin_ret, sin_mla = tables(jnp.sin(ang))
    cos_ret_ref[...] = cos_ret
    sin_ret_ref[...] = sin_ret * sign_ref[...]
    cos_mla_ref[...] = cos_mla
    sin_mla_ref[...] = sin_mla * sign_ref[...]


def _rope_tables(positions, *, tm):
    n = positions.size
    tm = min(tm, n)
    half_ret = HEAD_DIM // 2
    half_mla = MLA_ROPE // 2
    f_ret = ROPE_THETA ** (-jnp.arange(half_ret, dtype=F32) / half_ret)
    f_mla = ROPE_THETA ** (-jnp.arange(half_mla, dtype=F32) / half_mla)
    freq = jnp.concatenate([f_ret, f_mla, jnp.zeros((LANES - half_ret - half_mla,), F32)]).reshape(1, LANES)
    sign = jnp.concatenate([-jnp.ones((1, LANES // 2), F32), jnp.ones((1, LANES // 2), F32)], axis=1)
    tab = jax.ShapeDtypeStruct((n, LANES), F32)
    row = pl.BlockSpec((tm, LANES), lambda i: (i, 0))
    return pl.pallas_call(
        _rope_table_kernel,
        grid=(n // tm,),
        in_specs=[pl.BlockSpec((tm, 1), lambda i: (i, 0)), _resident((1, LANES)), _resident((1, LANES))],
        out_specs=[row, row, row, row],
        out_shape=[tab, tab, tab, tab],
        compiler_params=_params(("arbitrary",)),
        name="rope_tables",
    )(positions.reshape(n, 1), freq, sign)


def _in_proj_kernel(x_ref, w_ref, wgt_ref, gbias_ref, qg_ref, wuq_ref, kvg_ref, wukv_ref,
                    convw_ref, convb_ref, gateb_ref, cr_ref, sr_ref, cm_ref, sm_ref,
                    q_ref, k_ref, v_ref, mq_ref, mk_ref, mkt_ref, mv_ref, ig_ref, fg_ref, mo_ref,
                    rq_ref, rk_ref, rkt_ref, rv_ref, rg_ref, gate_ref, ext_ref, *, tm):
    xb = x_ref[0].astype(BF16)
    half = LANES // 2

    def proj(c0, c1):
        return _dot(xb, w_ref[:, c0:c1])

    def rope(t, cos, sin):
        return t * cos + pltpu.roll(t, half, axis=1) * sin

    cos_m, sin_m = cm_ref[...], sm_ref[...]
    cos_r, sin_r = cr_ref[...], sr_ref[...]

    cq = _rms_norm(proj(_C_CQ, _C_CKV), qg_ref[...]).astype(BF16)
    ckv = _rms_norm(proj(_C_CKV, _C_KR), kvg_ref[...]).astype(BF16)
    qh = _dot(cq, wuq_ref[...])
    kv = _dot(ckv, wukv_ref[...])
    sm_scale = (MLA_NOPE + MLA_ROPE) ** -0.5 * math.log2(math.e)
    pairs = [rope(qh[:, MIX_W + p * LANES:MIX_W + (p + 1) * LANES], cos_m, sin_m) * sm_scale
             for p in range(2)]
    for h in range(N_HEADS):
        q_ref[0, h, :, 0:LANES] = (qh[:, h * LANES:(h + 1) * LANES] * sm_scale).astype(BF16)
        q_ref[0, h, :, LANES:2 * LANES] = pairs[h // 2].astype(BF16)

    kr = rope(proj(_C_KR, _C_MQK), cos_m, sin_m)
    lane = lax.broadcasted_iota(jnp.int32, (tm, LANES), 1)
    group_odd = (lane // (MLA_ROPE // 2)) % 2 == 1
    kr_par = [jnp.where(group_odd, 0.0, kr).astype(BF16), jnp.where(group_odd, kr, 0.0).astype(BF16)]
    for h in range(N_HEADS):
        k_ref[0, h, :, 0:LANES] = kv[:, h * LANES:(h + 1) * LANES].astype(BF16)
        k_ref[0, h, :, LANES:2 * LANES] = kr_par[h % 2]
    v_ref[0] = kv[:, MIX_W:2 * MIX_W].astype(BF16)

    @pl.when(pl.program_id(1) == 0)
    def _():
        ext_ref[0:SUBLANES, :] = jnp.zeros((SUBLANES, 2 * MIX_W), F32)

    qk_pre = proj(_C_MQK, _C_MV)
    ext_ref[SUBLANES:SUBLANES + tm, :] = qk_pre
    conv = qk_pre * convw_ref[CONV_K - 1:CONV_K, :] + convb_ref[...]
    for lag in range(1, CONV_K):
        conv = conv + ext_ref[SUBLANES - lag:SUBLANES - lag + tm, :] * convw_ref[CONV_K - 1 - lag:CONV_K - lag, :]
    ext_ref[0:SUBLANES, :] = qk_pre[tm - SUBLANES:tm, :]
    conv = conv * _sigmoid(conv)
    mq_ref[0] = conv[:, 0:MIX_W].astype(BF16)
    mk = conv[:, MIX_W:2 * MIX_W] * HEAD_DIM ** -0.5
    mk_ref[0] = mk.astype(BF16)
    mkt_ref[0] = mk.T.astype(BF16)
    mv_ref[0] = proj(_C_MV, _C_MO).astype(BF16)
    mo_ref[0] = _sigmoid(proj(_C_MO, _C_RQ)).astype(BF16)

    gates_t = _dot_nt(wgt_ref[...], xb) + gbias_ref[...]
    for c in range(tm // CHUNK):
        ig_ref[0, c] = gates_t[0:SUBLANES, c * CHUNK:(c + 1) * CHUNK]
        fg_ref[0, c] = gates_t[SUBLANES:2 * SUBLANES, c * CHUNK:(c + 1) * CHUNK]

    rq = proj(_C_RQ, _C_RK)
    rk = proj(_C_RK, _C_RV)
    for h in range(N_HEADS):
        sl = slice(h * LANES, (h + 1) * LANES)
        rq_ref[0, :, sl] = rope(rq[:, sl], cos_r, sin_r).astype(BF16)
        rk_h = rope(rk[:, sl], cos_r, sin_r) * HEAD_DIM ** -0.5
        rk_ref[0, :, sl] = rk_h.astype(BF16)
        rkt_ref[0, sl, :] = rk_h.T.astype(BF16)
    rv_ref[0] = proj(_C_RV, _C_RG).astype(BF16)
    rg = proj(_C_RG, _C_GL)
    rg_ref[0] = (rg * _sigmoid(rg)).astype(BF16)

    for br in range(N_BRANCH):
        c0 = _C_GL + br * D_MODEL
        sl = slice(br * D_MODEL, (br + 1) * D_MODEL)
        gate_ref[0, :, sl] = _sigmoid(proj(c0, c0 + D_MODEL) + gateb_ref[:, sl]).astype(BF16)


def _in_proj(x, w_main, w_gates_t, gate_bias, q_norm_g, w_uq, kv_norm_g, w_ukv, conv_w, conv_b,
             gate_b, tables, *, tm):
    b, s, _ = x.shape
    tm = min(tm, s)
    nt = s // tm
    cos_r, sin_r, cos_m, sin_m = tables

    def tok(width):
        return pl.BlockSpec((1, tm, width), lambda i, j: (i, j, 0))

    head4 = pl.BlockSpec((1, N_HEADS, tm, MLA_QK), lambda i, j: (i, 0, j, 0))
    gate_t = pl.BlockSpec((1, tm // CHUNK, SUBLANES, CHUNK), lambda i, j: (i, j, 0, 0))
    tok_t = pl.BlockSpec((1, MIX_W, tm), lambda i, j: (i, 0, j))
    table = pl.BlockSpec((tm, LANES), lambda i, j: (i * nt + j, 0))
    act = lambda width: jax.ShapeDtypeStruct((b, s, width), BF16)
    qk4 = jax.ShapeDtypeStruct((b, N_HEADS, s, MLA_QK), BF16)
    gt = jax.ShapeDtypeStruct((b, s // CHUNK, SUBLANES, CHUNK), F32)
    act_t = jax.ShapeDtypeStruct((b, MIX_W, s), BF16)
    return pl.pallas_call(
        functools.partial(_in_proj_kernel, tm=tm),
        grid=(b, nt),
        in_specs=[tok(D_MODEL), _resident(w_main.shape), _resident(w_gates_t.shape),
                  _resident(gate_bias.shape), _resident(q_norm_g.shape), _resident(w_uq.shape),
                  _resident(kv_norm_g.shape), _resident(w_ukv.shape), _resident(conv_w.shape),
                  _resident(conv_b.shape), _resident(gate_b.shape), table, table, table, table],
        out_specs=[head4, head4, tok(MIX_W), tok(MIX_W), tok(MIX_W), tok_t, tok(MIX_W), gate_t, gate_t,
                   tok(MIX_W), tok(MIX_W), tok(MIX_W), tok_t, tok(MIX_W), tok(MIX_W),
                   tok(N_BRANCH * D_MODEL)],
        out_shape=[qk4, qk4, act(MIX_W), act(MIX_W), act(MIX_W), act_t, act(MIX_W), gt, gt,
                   act(MIX_W), act(MIX_W), act(MIX_W), act_t, act(MIX_W), act(MIX_W),
                   act(N_BRANCH * D_MODEL)],
        scratch_shapes=[pltpu.VMEM((tm + SUBLANES, 2 * MIX_W), F32)],
        compiler_params=_params(("arbitrary", "arbitrary")),
        name="in_proj",
    )(x, w_main, w_gates_t, gate_bias, q_norm_g, w_uq, kv_norm_g, w_ukv, conv_w, conv_b, gate_b,
      cos_r, sin_r, cos_m, sin_m)


def _attn_kernel(q_ref, k_ref, v_ref, o_ref, m_ref, acc_ref, *, tq, tk):
    i = pl.program_id(2)
    n_chain = tq // tk
    m_ref[...] = jnp.full(m_ref.shape, NEG_BIG, F32)
    acc_ref[...] = jnp.zeros(acc_ref.shape, F32)
    ones = jnp.ones((tk, HEAD_DIM), BF16)

    def chain_step(c, start, diagonal):
        rows = slice(c * tk, (c + 1) * tk)
        k = k_ref[0, 0, pl.ds(start, tk), :]
        v1 = jnp.concatenate([v_ref[0, pl.ds(start, tk), :], ones], axis=1)
        s = _dot_nt(q_ref[0, 0, rows, :], k)
        if diagonal:
            row = lax.broadcasted_iota(jnp.int32, (tk, tk), 0)
            col = lax.broadcasted_iota(jnp.int32, (tk, tk), 1)
            s = jnp.where(col <= row, s, NEG_BIG)
        blocks = [s[:, j * LANES:(j + 1) * LANES] for j in range(tk // LANES)]
        m_prev = m_ref[rows, :]
        m_blk = functools.reduce(jnp.maximum, blocks)
        m_new = jnp.maximum(m_prev, jnp.max(m_blk, axis=-1, keepdims=True))
        alpha = jnp.exp2(m_prev - m_new)
        p = jnp.concatenate([jnp.exp2(blk - m_new) for blk in blocks], axis=1).astype(BF16)
        acc_ref[rows, :] = jnp.concatenate([alpha, alpha], axis=1) * acc_ref[rows, :] + _dot(p, v1)
        m_ref[rows, :] = m_new

    unroll = 2 if n_chain % 2 == 0 else 1

    def body(j, carry):
        for u in range(unroll):
            start = pl.multiple_of((j * unroll + u) * tk, tk)
            for c in range(n_chain):
                chain_step(c, start, False)
        return carry

    lax.fori_loop(0, i * n_chain // unroll, body, 0)
    for c in range(n_chain):
        for jj in range(c + 1):
            chain_step(c, pl.multiple_of(i * tq + jj * tk, tk), jj == c)
    acc = acc_ref[...]
    o_ref[0] = (acc[:, 0:HEAD_DIM] / acc[:, HEAD_DIM:2 * HEAD_DIM]).astype(o_ref.dtype)


def _attention(q, k, v, *, tq, tk):
    b, h, s, _ = q.shape
    tq, tk = min(tq, s), min(tk, s)
    return pl.pallas_call(
        functools.partial(_attn_kernel, tq=tq, tk=tk),
        grid=(b, h, s // tq),
        in_specs=[pl.BlockSpec((1, 1, tq, MLA_QK), lambda bi, hi, i: (bi, hi, i, 0)),
                  pl.BlockSpec((1, 1, s, MLA_QK), lambda bi, hi, i: (bi, hi, 0, 0)),
                  pl.BlockSpec((1, s, HEAD_DIM), lambda bi, hi, i: (bi, 0, hi))],
        out_specs=pl.BlockSpec((1, tq, HEAD_DIM), lambda bi, hi, i: (bi, i, hi)),
        out_shape=jax.ShapeDtypeStruct((b, s, MIX_W), BF16),
        scratch_shapes=[pltpu.VMEM((tq, LANES), F32), pltpu.VMEM((tq, 2 * HEAD_DIM), F32)],
        compiler_params=_params(("arbitrary", "arbitrary", "arbitrary")),
        name="mla_attention",
    )(q, k, v)


def _lane_scan(x, combine, identity):
    lane = lax.broadcasted_iota(jnp.int32, x.shape, 1)
    shift = 1
    while shift < LANES:
        x = combine(x, jnp.where(lane >= shift, pltpu.roll(x, shift, axis=1), identity))
        shift *= 2
    return x


def _gate_prep_kernel(ig_ref, fg_ref, r_ref, eloc_ref, a_ref, b_ref, cols_ref, mprev_ref, stack_ref):
    nc = ig_ref.shape[1]
    rows = nc * SUBLANES
    L = CHUNK
    ig = ig_ref[0].reshape(rows, L)
    fg = fg_ref[0].reshape(rows, L)
    lf = jnp.minimum(fg, 0.0) - jnp.log(1.0 + jnp.exp(-jnp.abs(fg)))
    g = _lane_scan(lf, jnp.add, 0.0)
    g_last = jnp.broadcast_to(g[:, L - 1:L], (rows, L))
    r = ig - g
    w_loc = g_last + r
    m_loc = jnp.broadcast_to(jnp.max(w_loc, axis=-1, keepdims=True), (rows, L))
    eloc_ref[0] = jnp.exp(w_loc - m_loc).reshape(nc, SUBLANES, L)
    r_ref[0] = r.reshape(nc, SUBLANES, L)
    cmax = _lane_scan(r, jnp.maximum, NEG_BIG)

    m = jnp.zeros((SUBLANES, L), F32)
    for c in range(nc):
        sl = slice(c * SUBLANES, (c + 1) * SUBLANES)
        mprev_ref[sl, :] = m
        m = jnp.maximum(g_last[sl, :] + m, m_loc[sl, :])
    m_prev = mprev_ref[...]
    mx = jnp.maximum(m_prev, cmax)
    m_new = jnp.maximum(g_last + m_prev, m_loc)
    a_ref[0] = jnp.exp(g_last + m_prev - m_new).reshape(nc, SUBLANES, L)
    b_ref[0] = jnp.exp(m_loc - m_new).reshape(nc, SUBLANES, L)

    stack_ref[...] = jnp.zeros(stack_ref.shape, F32)
    stack_ref[:, 0:SUBLANES, :] = mx.reshape(nc, SUBLANES, L)
    stack_ref[:, SUBLANES:2 * SUBLANES, :] = jnp.exp(m_prev - mx).reshape(nc, SUBLANES, L)
    stack_ref[:, 2 * SUBLANES:3 * SUBLANES, :] = jnp.exp(-g - mx).reshape(nc, SUBLANES, L)

    def body(c, carry):
        cols_ref[0, c] = stack_ref[c].T
        return carry

    lax.fori_loop(0, nc, body, 0)


def _gate_prep(ig, fg):
    b, nc, _, _ = ig.shape
    rows = pl.BlockSpec((1, nc, SUBLANES, CHUNK), lambda i: (i, 0, 0, 0))
    cols = pl.BlockSpec((1, nc, CHUNK, LANES), lambda i: (i, 0, 0, 0))
    row_shape = jax.ShapeDtypeStruct((b, nc, SUBLANES, CHUNK), F32)
    return pl.pallas_call(
        _gate_prep_kernel,
        grid=(b,),
        in_specs=[rows, rows],
        out_specs=[rows, rows, rows, rows, cols],
        out_shape=[row_shape, row_shape, row_shape, row_shape,
                   jax.ShapeDtypeStruct((b, nc, CHUNK, LANES), F32)],
        scratch_shapes=[pltpu.VMEM((nc * SUBLANES, CHUNK), F32), pltpu.VMEM((nc, CHUNK, LANES), F32)],
        compiler_params=_params(("arbitrary",)),
        name="mlstm_gate_prep",
    )(ig, fg)


def _head_norm_gate(hh, g_row, gate):
    mu = jnp.mean(hh, axis=-1, keepdims=True)
    hc = hh - mu
    var = jnp.mean(hc * hc, axis=-1, keepdims=True)
    return hc * lax.rsqrt(var + NORM_EPS) * g_row * gate


def _recurrent_kernel(mq_ref, mk_ref, mkt_ref, mv_ref, r_ref, eloc_ref, a_ref, b_ref, cols_ref, mo_ref,
                      rq_ref, rk_ref, rkt_ref, rv_ref, rg_ref, mg_ref, rgn_ref, yml_ref, yrt_ref,
                      c_ref, r_state_ref, dec_ref):
    L = CHUNK
    row = lax.broadcasted_iota(jnp.int32, (L, L), 0)
    col = lax.broadcasted_iota(jnp.int32, (L, L), 1)
    causal = col <= row

    @pl.when(pl.program_id(0) == 0)
    def _():
        c_ref[...] = jnp.zeros(c_ref.shape, F32)
        r_state_ref[...] = jnp.zeros(r_state_ref.shape, F32)
        rowf = row.astype(F32)
        colf = col.astype(F32)
        for h in range(N_HEADS):
            lg = math.log(1.0 - 2.0 ** (-5.0 - h))
            dec_ref[h, 0] = jnp.where(causal, jnp.exp(jnp.where(causal, rowf - colf, 0.0) * lg), 0.0)
            dec_ref[h, 1] = jnp.exp((rowf + 1.0) * lg)
            dec_ref[h, 2] = jnp.exp((L - 1.0 - colf) * lg)

    ones = jnp.ones((L, HEAD_DIM), BF16)
    chains = [(bi, h) for bi in range(mq_ref.shape[0]) for h in range(N_HEADS)]
    head = lambda h: slice(h * HEAD_DIM, (h + 1) * HEAD_DIM)
    cols = [cols_ref[bi, 0] for bi in range(mq_ref.shape[0])]

    ml_scores = [_dot_nt(mq_ref[bi, :, head(h)], mk_ref[bi, :, head(h)]) for bi, h in chains]
    rt_scores = [_dot_nt(rq_ref[bi, :, head(h)], rk_ref[bi, :, head(h)]) for bi, h in chains]
    ml_inter = [_dot(mq_ref[bi, :, head(h)], c_ref[bi * N_HEADS + h].astype(BF16)) for bi, h in chains]
    rt_inter = [_dot(rq_ref[bi, :, head(h)], r_state_ref[bi * N_HEADS + h].astype(BF16)) for bi, h in chains]
    v1s = [jnp.concatenate([mv_ref[bi, :, head(h)], ones], axis=1) for bi, h in chains]

    ml_qk = []
    for (bi, h), s in zip(chains, ml_scores):
        p = jnp.exp(jnp.where(causal, r_ref[bi, 0][h:h + 1, :] - cols[bi][:, h:h + 1], NEG_BIG))
        ml_qk.append((s * p).astype(BF16))
    rt_qk = [(s * dec_ref[h, 0]).astype(BF16) for (bi, h), s in zip(chains, rt_scores)]

    ml_intra = [_dot(qk, v1) for qk, v1 in zip(ml_qk, v1s)]
    rt_intra = [_dot(qk, rv_ref[bi, :, head(h)]) for (bi, h), qk in zip(chains, rt_qk)]
    ml_kwt = [(mkt_ref[bi, head(h), :].astype(F32) * eloc_ref[bi, 0][h:h + 1, :]).astype(BF16)
              for bi, h in chains]
    rt_kwt = [(rkt_ref[bi, head(h), :].astype(F32) * dec_ref[h, 2]).astype(BF16) for bi, h in chains]
    ml_upd = [_dot(kwt, v1) for kwt, v1 in zip(ml_kwt, v1s)]
    rt_upd = [_dot(kwt, rv_ref[bi, :, head(h)]) for (bi, h), kwt in zip(chains, rt_kwt)]

    for i, (bi, h) in enumerate(chains):
        st = bi * N_HEADS + h
        e_inter_c = cols[bi][:, SUBLANES + h:SUBLANES + h + 1]
        e_floor_c = cols[bi][:, 2 * SUBLANES + h:2 * SUBLANES + h + 1]
        nd = ml_intra[i] + e_inter_c * ml_inter[i]
        hh = nd[:, 0:HEAD_DIM] / jnp.maximum(jnp.abs(nd[:, HEAD_DIM:2 * HEAD_DIM]), e_floor_c)
        yml_ref[bi, :, head(h)] = _head_norm_gate(
            hh, mg_ref[:, head(h)], mo_ref[bi, :, head(h)].astype(F32)).astype(BF16)
        yrt_ref[bi, :, head(h)] = _head_norm_gate(
            rt_intra[i] + dec_ref[h, 1] * rt_inter[i], rgn_ref[:, head(h)],
            rg_ref[bi, :, head(h)].astype(F32)).astype(BF16)
        c_ref[st] = a_ref[bi, 0][h:h + 1, 0:1] * c_ref[st] + b_ref[bi, 0][h:h + 1, 0:1] * ml_upd[i]
        g_chunk = math.exp(L * math.log(1.0 - 2.0 ** (-5.0 - h)))
        r_state_ref[st] = g_chunk * r_state_ref[st] + rt_upd[i]


def _recurrent(mq, mk, mkt, mv, r, eloc, a, bsc, cols, mo, rq, rk, rkt, rv, rg, mlstm_norm_g, ret_norm_g):
    b, s, _ = mq.shape
    tok = pl.BlockSpec((b, CHUNK, MIX_W), lambda c: (0, c, 0))
    tok_t = pl.BlockSpec((b, MIX_W, CHUNK), lambda c: (0, 0, c))
    rows = pl.BlockSpec((b, 1, SUBLANES, CHUNK), lambda c: (0, c, 0, 0))
    colsp = pl.BlockSpec((b, 1, CHUNK, LANES), lambda c: (0, c, 0, 0))
    out = jax.ShapeDtypeStruct((b, s, MIX_W), BF16)
    return pl.pallas_call(
        _recurrent_kernel,
        grid=(s // CHUNK,),
        in_specs=[tok, tok, tok_t, tok, rows, rows, rows, rows, colsp, tok,
                  tok, tok, tok_t, tok, tok, _resident((1, MIX_W)), _resident((1, MIX_W))],
        out_specs=[tok, tok],
        out_shape=[out, out],
        scratch_shapes=[pltpu.VMEM((b * N_HEADS, HEAD_DIM, 2 * HEAD_DIM), F32),
                        pltpu.VMEM((b * N_HEADS, HEAD_DIM, HEAD_DIM), F32),
                        pltpu.VMEM((N_HEADS, 3, CHUNK, CHUNK), F32)],
        compiler_params=_params(("arbitrary",)),
        name="recurrent_mixers",
    )(mq, mk, mkt, mv, r, eloc, a, bsc, cols, mo, rq, rk, rkt, rv, rg, mlstm_norm_g, ret_norm_g)


def _merge_kernel(x_ref, ya_ref, yb_ref, yc_ref, gate_ref, wa_ref, wb_ref, wc_ref, wo_ref, g_ref, b_ref,
                  o_ref, *, alpha, sub):
    for r0 in range(0, x_ref.shape[0], sub):
        rows = slice(r0, r0 + sub)
        merged = None
        for br, (y_ref, w_ref) in enumerate(((ya_ref, wa_ref), (yb_ref, wb_ref), (yc_ref, wc_ref))):
            gate = gate_ref[rows, br * D_MODEL:(br + 1) * D_MODEL].astype(F32)
            term = gate * _dot(y_ref[rows, :], w_ref[...])
            merged = term if merged is None else merged + term
        mix = _dot(merged.astype(BF16), wo_ref[...])
        o_ref[rows, :] = _layer_norm(alpha * x_ref[rows, :] + mix, g_ref[...], b_ref[...])


def _merge(x, y_mla, y_ml, y_rt, gate, w_a, w_b, w_c, w_out, g, b, *, alpha, tm, sub):
    n = x.shape[0]
    tm = min(tm, n)
    sub = min(sub, tm)
    tok = lambda width: pl.BlockSpec((tm, width), lambda i: (i, 0))
    return pl.pallas_call(
        functools.partial(_merge_kernel, alpha=alpha, sub=sub),
        grid=(n // tm,),
        in_specs=[tok(D_MODEL), tok(MIX_W), tok(MIX_W), tok(MIX_W), tok(N_BRANCH * D_MODEL),
                  _resident((MIX_W, D_MODEL)), _resident((MIX_W, D_MODEL)), _resident((MIX_W, D_MODEL)),
                  _resident((D_MODEL, D_MODEL)), _resident((1, D_MODEL)), _resident((1, D_MODEL))],
        out_specs=tok(D_MODEL),
        out_shape=jax.ShapeDtypeStruct((n, D_MODEL), F32),
        compiler_params=_params(("arbitrary",)),
        name="merge_out_ln",
    )(x, y_mla, y_ml, y_rt, gate, w_a, w_b, w_c, w_out, g, b)


def _pack_w_in(w_in):
    o = 0
    parts = {}
    for name, width in (("cq", MLA_Q_LORA), ("ckv", MLA_KV_LORA), ("kr", MLA_ROPE), ("mq", MIX_W),
                        ("mk", MIX_W), ("mv", MIX_W), ("mi", N_HEADS), ("mf", N_HEADS), ("mo", MIX_W),
                        ("rq", MIX_W), ("rk", MIX_W), ("rv", MIX_W), ("rg", MIX_W),
                        ("gl", N_BRANCH * D_MODEL)):
        parts[name] = w_in[:, o:o + width]
        o += width
    hr = MLA_ROPE // 2
    kr1, kr2 = parts["kr"][:, :hr], parts["kr"][:, hr:]
    main = jnp.concatenate(
        [parts["cq"], parts["ckv"], kr1, kr1, kr2, kr2, parts["mq"], parts["mk"], parts["mv"],
         parts["mo"], parts["rq"], parts["rk"], parts["rv"], parts["rg"], parts["gl"]], axis=1)
    gates_t = jnp.concatenate([parts["mi"], parts["mi"], parts["mf"], parts["mf"]], axis=1).T
    return main.astype(BF16), gates_t.astype(BF16)


def _pack_w_uq(w_uq):
    per = MLA_NOPE + MLA_ROPE
    hr = MLA_ROPE // 2
    nope = [w_uq[:, h * per:h * per + MLA_NOPE] for h in range(N_HEADS)]
    x1 = [w_uq[:, h * per + MLA_NOPE:h * per + MLA_NOPE + hr] for h in range(N_HEADS)]
    x2 = [w_uq[:, h * per + MLA_NOPE + hr:(h + 1) * per] for h in range(N_HEADS)]
    pairs = [x1[0], x1[1], x2[0], x2[1], x1[2], x1[3], x2[2], x2[3]]
    return jnp.concatenate(nope + pairs, axis=1).astype(BF16)


def _pack_w_ukv(w_ukv):
    per = MLA_NOPE + HEAD_DIM
    keys = [w_ukv[:, h * per:h * per + MLA_NOPE] for h in range(N_HEADS)]
    vals = [w_ukv[:, h * per + MLA_NOPE:(h + 1) * per] for h in range(N_HEADS)]
    return jnp.concatenate(keys + vals, axis=1).astype(BF16)


def kernel(x, positions, ffn1_w_gate, ffn1_w_up, ffn1_w_down, ln1_g, ln1_b, w_in, mla_q_norm_g, mla_w_uq, mla_kv_norm_g, mla_w_ukv, mlstm_conv_w, mlstm_conv_b, mlstm_gate_b, mlstm_norm_g, ret_norm_g, w_br_mla, w_br_mlstm, w_br_ret, gate_b, w_out, ln2_g, ln2_b, ffn2_w_gate, ffn2_w_up, ffn2_w_down, ln3_g, ln3_b):
    b, s, d = x.shape
    depth = w_in.shape[0]
    alpha = (2 * depth) ** DEPTH_ALPHA_POW
    n = b * s
    tables = _rope_tables(positions, tm=ROPE_TILE)
    row = lambda a: a.reshape(1, -1)
    ff_split = (D_FF // MXU_WIDTH + 1) // 2 * MXU_WIDTH
    ffn = functools.partial(_ffn_ln, alpha=alpha, tm=TOKEN_TILE, sub=ROW_BLOCK,
                            ff_chunks=((0, ff_split), (ff_split, D_FF)))

    h = x.reshape(n, d)
    for l in range(depth):
        h = ffn(h, ffn1_w_gate, ffn1_w_up, ffn1_w_down, row(ln1_g[l]), row(ln1_b[l]), layer=l)

        w_main, w_gates_t = _pack_w_in(w_in[l])
        gbias = mlstm_gate_b[l]
        gate_bias = jnp.concatenate([gbias[:N_HEADS], gbias[:N_HEADS], gbias[N_HEADS:], gbias[N_HEADS:]]
                                    ).reshape(2 * SUBLANES, 1)
        (q, k, v, mq, mk, mkt, mv, ig, fg, mo, rq, rk, rkt, rv, rg, gate) = _in_proj(
            h.reshape(b, s, d), w_main, w_gates_t, gate_bias, row(mla_q_norm_g[l]), _pack_w_uq(mla_w_uq[l]),
            row(mla_kv_norm_g[l]), _pack_w_ukv(mla_w_ukv[l]), mlstm_conv_w[l], row(mlstm_conv_b[l]),
            row(gate_b[l]), tables, tm=IN_PROJ_TILE)
        y_mla = _attention(q, k, v, tq=ATTN_Q_TILE, tk=ATTN_K_TILE)
        r, eloc, a_sc, b_sc, cols = _gate_prep(ig, fg)
        y_ml, y_rt = _recurrent(mq, mk, mkt, mv, r, eloc, a_sc, b_sc, cols, mo, rq, rk, rkt, rv, rg,
                                row(mlstm_norm_g[l]), row(ret_norm_g[l]))
        h = _merge(h, y_mla.reshape(n, MIX_W), y_ml.reshape(n, MIX_W), y_rt.reshape(n, MIX_W),
                   gate.reshape(n, N_BRANCH * D_MODEL), w_br_mla[l].astype(BF16), w_br_mlstm[l].astype(BF16),
                   w_br_ret[l].astype(BF16), w_out[l].astype(BF16), row(ln2_g[l]), row(ln2_b[l]),
                   alpha=alpha, tm=TOKEN_TILE, sub=ROW_BLOCK)

        h = ffn(h, ffn2_w_gate, ffn2_w_up, ffn2_w_down, row(ln3_g[l]), row(ln3_b[l]), layer=l)
    return h.reshape(b, s, d)
```

```python
import functools
import math

import jax
import jax.numpy as jnp
from jax import lax
from jax.experimental import pallas as pl
from jax.experimental.pallas import tpu as pltpu

D_MODEL = 1024
D_FF = 2816
N_HEADS = 4
HEAD_DIM = 128
MIX_W = N_HEADS * HEAD_DIM
MLA_Q_LORA = 384
MLA_KV_LORA = 256
MLA_NOPE = 128
MLA_ROPE = 64
MLA_QK = 2 * HEAD_DIM
CONV_K = 4
N_BRANCH = 3
CHUNK = 128
ROPE_THETA = 10000.0
NORM_EPS = 1e-5
NEG_BIG = -1e30
DEPTH_ALPHA_POW = 0.25

LANES = 128
SUBLANES = 8
MXU_WIDTH = 256
VMEM_LIMIT_BYTES = 56 * 1024 * 1024

ROW_BLOCK = 2 * MXU_WIDTH
TOKEN_TILE = 2 * ROW_BLOCK
IN_PROJ_TILE = ROW_BLOCK
ATTN_K_TILE = ROW_BLOCK
ATTN_Q_TILE = 8 * ATTN_K_TILE
ROPE_TILE = 1024

_C_CQ = 0
_C_CKV = _C_CQ + MLA_Q_LORA
_C_KR = _C_CKV + MLA_KV_LORA
_C_MQK = _C_KR + LANES
_C_MV = _C_MQK + 2 * MIX_W
_C_MO = _C_MV + MIX_W
_C_RQ = _C_MO + MIX_W
_C_RK = _C_RQ + MIX_W
_C_RV = _C_RK + MIX_W
_C_RG = _C_RV + MIX_W
_C_GL = _C_RG + MIX_W
_C_END = _C_GL + N_BRANCH * D_MODEL

BF16 = jnp.bfloat16
F32 = jnp.float32


def _resident(shape):
    nd = len(shape)
    return pl.BlockSpec(shape, lambda *_: (0,) * nd, pipeline_mode=pl.Buffered(1))


def _params(sem):
    return pltpu.CompilerParams(dimension_semantics=sem, vmem_limit_bytes=VMEM_LIMIT_BYTES)


def _layer_norm(y, g, b):
    mu = jnp.mean(y, axis=-1, keepdims=True)
    yc = y - mu
    var = jnp.mean(yc * yc, axis=-1, keepdims=True)
    return yc * lax.rsqrt(var + NORM_EPS) * g + b


def _rms_norm(y, g):
    return y * lax.rsqrt(jnp.mean(y * y, axis=-1, keepdims=True) + NORM_EPS) * g


def _sigmoid(x):
    return 1.0 / (1.0 + jnp.exp(-x))


def _dot(a, b):
    return jnp.dot(a, b, preferred_element_type=F32)


def _dot_nt(a, b):
    return lax.dot_general(a, b, (((1,), (1,)), ((), ())), preferred_element_type=F32)


FFN_LOAD_CHUNKS = 8


def _stage_weights_bf16(jobs, sem_ref):
    copies = []
    uses = {}
    for src, stage, sem0, dst in jobs:
        slot = uses.get(id(stage), 0) % 2
        uses[id(stage)] = uses.get(id(stage), 0) + 1
        copies.append((pltpu.make_async_copy(src, stage.at[slot], sem_ref.at[sem0 + slot]), stage, slot, dst))
    copies[0][0].start()
    for i, (copy, stage, slot, dst) in enumerate(copies):
        if i + 1 < len(copies):
            copies[i + 1][0].start()
        copy.wait()
        dst[...] = stage[slot].astype(BF16)


def _ffn_ln_kernel(x_ref, wg_hbm, wu_hbm, wd_hbm, g_ref, b_ref, o_ref,
                   wg_ref, wu_ref, wd_ref, stage_up_ref, stage_dn_ref, sem_ref, *, layer, alpha, sub, ff_chunks):
    @pl.when(pl.program_id(0) == 0)
    def _():
        up_rows = D_MODEL // FFN_LOAD_CHUNKS
        dn_rows = D_FF // FFN_LOAD_CHUNKS
        jobs = []
        for w_hbm, w_ref in ((wg_hbm, wg_ref), (wu_hbm, wu_ref)):
            for c in range(FFN_LOAD_CHUNKS):
                rows = pl.ds(c * up_rows, up_rows)
                jobs.append((w_hbm.at[layer, rows, :], stage_up_ref, 0, w_ref.at[rows, :]))
        for c in range(FFN_LOAD_CHUNKS):
            rows = pl.ds(c * dn_rows, dn_rows)
            jobs.append((wd_hbm.at[layer, rows, :], stage_dn_ref, 2, wd_ref.at[rows, :]))
        _stage_weights_bf16(jobs, sem_ref)

    for r0 in range(0, x_ref.shape[0], sub):
        rows = slice(r0, r0 + sub)
        x = x_ref[rows, :]
        xb = x.astype(BF16)
        acc = None
        for c0, c1 in ff_chunks:
            gate = _dot(xb, wg_ref[:, c0:c1])
            up = _dot(xb, wu_ref[:, c0:c1])
            act = (gate * _sigmoid(gate) * up).astype(BF16)
            part = _dot(act, wd_ref[c0:c1, :])
            acc = part if acc is None else acc + part
        o_ref[rows, :] = _layer_norm(alpha * x + 0.5 * acc, g_ref[...], b_ref[...])


def _ffn_ln(x, wg, wu, wd, g, b, *, layer, alpha, tm, sub, ff_chunks):
    n = x.shape[0]
    tm = min(tm, n)
    sub = min(sub, tm)
    row = pl.BlockSpec((tm, D_MODEL), lambda i: (i, 0))
    hbm = pl.BlockSpec(memory_space=pl.ANY)
    return pl.pallas_call(
        functools.partial(_ffn_ln_kernel, layer=layer, alpha=alpha, sub=sub, ff_chunks=ff_chunks),
        grid=(n // tm,),
        in_specs=[row, hbm, hbm, hbm, _resident((1, D_MODEL)), _resident((1, D_MODEL))],
        out_specs=row,
        out_shape=jax.ShapeDtypeStruct((n, D_MODEL), F32),
        scratch_shapes=[pltpu.VMEM((D_MODEL, D_FF), BF16), pltpu.VMEM((D_MODEL, D_FF), BF16),
                        pltpu.VMEM((D_FF, D_MODEL), BF16),
                        pltpu.VMEM((2, D_MODEL // FFN_LOAD_CHUNKS, D_FF), F32),
                        pltpu.VMEM((2, D_FF // FFN_LOAD_CHUNKS, D_MODEL), F32),
                        pltpu.SemaphoreType.DMA((4,))],
        compiler_params=_params(("arbitrary",)),
        name="ffn_ln",
    )(x, wg, wu, wd, g, b)


def _rope_table_kernel(pos_ref, freq_ref, sign_ref, cos_ret_ref, sin_ret_ref, cos_mla_ref, sin_mla_ref):
    ang = pos_ref[...].astype(F32) * freq_ref[...]
    lane = lax.broadcasted_iota(jnp.int32, ang.shape, 1)
    half, quarter = LANES // 2, LANES // 4

    def tables(t):
        ret = jnp.where(lane < half, t, pltpu.roll(t, half, axis=1))
        mla = pltpu.roll(t, half, axis=1)
        mla = jnp.where(lane < quarter, mla, pltpu.roll(mla, quarter, axis=1))
        return ret, jnp.where(lane < half, mla, pltpu.roll(mla, half, axis=1))

    cos_ret, cos_mla = tables(jnp.cos(ang))
    sin_ret, sin_mla = tables(jnp.sin(ang))
    cos_ret_ref[...] = cos_ret
    sin_ret_ref[...] = sin_ret * sign_ref[...]
    cos_mla_ref[...] = cos_mla
    sin_mla_ref[...] = sin_mla * sign_ref[...]


def _rope_tables(positions, *, tm):
    n = positions.size
    tm = min(tm, n)
    half_ret = HEAD_DIM // 2
    half_mla = MLA_ROPE // 2
    f_ret = ROPE_THETA ** (-jnp.arange(half_ret, dtype=F32) / half_ret)
    f_mla = ROPE_THETA ** (-jnp.arange(half_mla, dtype=F32) / half_mla)
    freq = jnp.concatenate([f_ret, f_mla, jnp.zeros((LANES - half_ret - half_mla,), F32)]).reshape(1, LANES)
    sign = jnp.concatenate([-jnp.ones((1, LANES // 2), F32), jnp.ones((1, LANES // 2), F32)], axis=1)
    tab = jax.ShapeDtypeStruct((n, LANES), F32)
    row = pl.BlockSpec((tm, LANES), lambda i: (i, 0))
    return pl.pallas_call(
        _rope_table_kernel,
        grid=(n // tm,),
        in_specs=[pl.BlockSpec((tm, 1), lambda i: (i, 0)), _resident((1, LANES)), _resident((1, LANES))],
        out_specs=[row, row, row, row],
        out_shape=[tab, tab, tab, tab],
        compiler_params=_params(("arbitrary",)),
        name="rope_tables",
    )(positions.reshape(n, 1), freq, sign)


def _in_proj_kernel(x_ref, w_ref, wgt_ref, gbias_ref, qg_ref, wuq_ref, kvg_ref, wukv_ref,
                    convw_ref, convb_ref, gateb_ref, cr_ref, sr_ref, cm_ref, sm_ref,
                    q_ref, k_ref, v_ref, mq_ref, mk_ref, mkt_ref, mv_ref, ig_ref, fg_ref, mo_ref,
                    rq_ref, rk_ref, rkt_ref, rv_ref, rg_ref, gate_ref, ext_ref, *, tm):
    xb = x_ref[0].astype(BF16)
    half = LANES // 2

    def proj(c0, c1):
        return _dot(xb, w_ref[:, c0:c1])

    def rope(t, cos, sin):
        return t * cos + pltpu.roll(t, half, axis=1) * sin

    cos_m, sin_m = cm_ref[...], sm_ref[...]
    cos_r, sin_r = cr_ref[...], sr_ref[...]

    cq = _rms_norm(proj(_C_CQ, _C_CKV), qg_ref[...]).astype(BF16)
    ckv = _rms_norm(proj(_C_CKV, _C_KR), kvg_ref[...]).astype(BF16)
    qh = _dot(cq, wuq_ref[...])
    kv = _dot(ckv, wukv_ref[...])
    sm_scale = (MLA_NOPE + MLA_ROPE) ** -0.5 * math.log2(math.e)
    pairs = [rope(qh[:, MIX_W + p * LANES:MIX_W + (p + 1) * LANES], cos_m, sin_m) * sm_scale
             for p in range(2)]
    for h in range(N_HEADS):
        q_ref[0, h, :, 0:LANES] = (qh[:, h * LANES:(h + 1) * LANES] * sm_scale).astype(BF16)
        q_ref[0, h, :, LANES:2 * LANES] = pairs[h // 2].astype(BF16)

    kr = rope(proj(_C_KR, _C_MQK), cos_m, sin_m)
    lane = lax.broadcasted_iota(jnp.int32, (tm, LANES), 1)
    group_odd = (lane // (MLA_ROPE // 2)) % 2 == 1
    kr_par = [jnp.where(group_odd, 0.0, kr).astype(BF16), jnp.where(group_odd, kr, 0.0).astype(BF16)]
    for h in range(N_HEADS):
        k_ref[0, h, :, 0:LANES] = kv[:, h * LANES:(h + 1) * LANES].astype(BF16)
        k_ref[0, h, :, LANES:2 * LANES] = kr_par[h % 2]
    v_ref[0] = kv[:, MIX_W:2 * MIX_W].astype(BF16)

    @pl.when(pl.program_id(1) == 0)
    def _():
        ext_ref[0:SUBLANES, :] = jnp.zeros((SUBLANES, 2 * MIX_W), F32)

    qk_pre = proj(_C_MQK, _C_MV)
    ext_ref[SUBLANES:SUBLANES + tm, :] = qk_pre
    conv = qk_pre * convw_ref[CONV_K - 1:CONV_K, :] + convb_ref[...]
    for lag in range(1, CONV_K):
        conv = conv + ext_ref[SUBLANES - lag:SUBLANES - lag + tm, :] * convw_ref[CONV_K - 1 - lag:CONV_K - lag, :]
    ext_ref[0:SUBLANES, :] = qk_pre[tm - SUBLANES:tm, :]
    conv = conv * _sigmoid(conv)
    mq_ref[0] = conv[:, 0:MIX_W].astype(BF16)
    mk = conv[:, MIX_W:2 * MIX_W] * HEAD_DIM ** -0.5
    mk_ref[0] = mk.astype(BF16)
    mkt_ref[0] = mk.T.astype(BF16)
    mv_ref[0] = proj(_C_MV, _C_MO).astype(BF16)
    mo_ref[0] = _sigmoid(proj(_C_MO, _C_RQ)).astype(BF16)

    gates_t = _dot_nt(wgt_ref[...], xb) + gbias_ref[...]
    for c in range(tm // CHUNK):
        ig_ref[0, c] = gates_t[0:SUBLANES, c * CHUNK:(c + 1) * CHUNK]
        fg_ref[0, c] = gates_t[SUBLANES:2 * SUBLANES, c * CHUNK:(c + 1) * CHUNK]

    rq = proj(_C_RQ, _C_RK)
    rk = proj(_C_RK, _C_RV)
    for h in range(N_HEADS):
        sl = slice(h * LANES, (h + 1) * LANES)
        rq_ref[0, :, sl] = rope(rq[:, sl], cos_r, sin_r).astype(BF16)
        rk_h = rope(rk[:, sl], cos_r, sin_r) * HEAD_DIM ** -0.5
        rk_ref[0, :, sl] = rk_h.astype(BF16)
        rkt_ref[0, sl, :] = rk_h.T.astype(BF16)
    rv_ref[0] = proj(_C_RV, _C_RG).astype(BF16)
    rg = proj(_C_RG, _C_GL)
    rg_ref[0] = (rg * _sigmoid(rg)).astype(BF16)

    for br in range(N_BRANCH):
        c0 = _C_GL + br * D_MODEL
        sl = slice(br * D_MODEL, (br + 1) * D_MODEL)
        gate_ref[0, :, sl] = _sigmoid(proj(c0, c0 + D_MODEL) + gateb_ref[:, sl]).astype(BF16)


def _in_proj(x, w_main, w_gates_t, gate_bias, q_norm_g, w_uq, kv_norm_g, w_ukv, conv_w, conv_b,
             gate_b, tables, *, tm):
    b, s, _ = x.shape
    tm = min(tm, s)
    nt = s // tm
    cos_r, sin_r, cos_m, sin_m = tables

    def tok(width):
        return pl.BlockSpec((1, tm, width), lambda i, j: (i, j, 0))

    head4 = pl.BlockSpec((1, N_HEADS, tm, MLA_QK), lambda i, j: (i, 0, j, 0))
    gate_t = pl.BlockSpec((1, tm // CHUNK, SUBLANES, CHUNK), lambda i, j: (i, j, 0, 0))
    tok_t = pl.BlockSpec((1, MIX_W, tm), lambda i, j: (i, 0, j))
    table = pl.BlockSpec((tm, LANES), lambda i, j: (i * nt + j, 0))
    act = lambda width: jax.ShapeDtypeStruct((b, s, width), BF16)
    qk4 = jax.ShapeDtypeStruct((b, N_HEADS, s, MLA_QK), BF16)
    gt = jax.ShapeDtypeStruct((b, s // CHUNK, SUBLANES, CHUNK), F32)
    act_t = jax.ShapeDtypeStruct((b, MIX_W, s), BF16)
    return pl.pallas_call(
        functools.partial(_in_proj_kernel, tm=tm),
        grid=(b, nt),
        in_specs=[tok(D_MODEL), _resident(w_main.shape), _resident(w_gates_t.shape),
                  _resident(gate_bias.shape), _resident(q_norm_g.shape), _resident(w_uq.shape),
                  _resident(kv_norm_g.shape), _resident(w_ukv.shape), _resident(conv_w.shape),
                  _resident(conv_b.shape), _resident(gate_b.shape), table, table, table, table],
        out_specs=[head4, head4, tok(MIX_W), tok(MIX_W), tok(MIX_W), tok_t, tok(MIX_W), gate_t, gate_t,
                   tok(MIX_W), tok(MIX_W), tok(MIX_W), tok_t, tok(MIX_W), tok(MIX_W),
                   tok(N_BRANCH * D_MODEL)],
        out_shape=[qk4, qk4, act(MIX_W), act(MIX_W), act(MIX_W), act_t, act(MIX_W), gt, gt,
                   act(MIX_W), act(MIX_W), act(MIX_W), act_t, act(MIX_W), act(MIX_W),
                   act(N_BRANCH * D_MODEL)],
        scratch_shapes=[pltpu.VMEM((tm + SUBLANES, 2 * MIX_W), F32)],
        compiler_params=_params(("arbitrary", "arbitrary")),
        name="in_proj",
    )(x, w_main, w_gates_t, gate_bias, q_norm_g, w_uq, kv_norm_g, w_ukv, conv_w, conv_b, gate_b,
      cos_r, sin_r, cos_m, sin_m)


def _attn_kernel(q_ref, k_ref, v_ref, o_ref, m_ref, acc_ref, *, tq, tk):
    i = pl.program_id(2)
    n_chain = tq // tk
    m_ref[...] = jnp.full(m_ref.shape, NEG_BIG, F32)
    acc_ref[...] = jnp.zeros(acc_ref.shape, F32)
    ones = jnp.ones((tk, HEAD_DIM), BF16)

    def chain_step(c, start, diagonal):
        rows = slice(c * tk, (c + 1) * tk)
        k = k_ref[0, 0, pl.ds(start, tk), :]
        v1 = jnp.concatenate([v_ref[0, pl.ds(start, tk), :], ones], axis=1)
        s = _dot_nt(q_ref[0, 0, rows, :], k)
        if diagonal:
            row = lax.broadcasted_iota(jnp.int32, (tk, tk), 0)
            col = lax.broadcasted_iota(jnp.int32, (tk, tk), 1)
            s = jnp.where(col <= row, s, NEG_BIG)
        blocks = [s[:, j * LANES:(j + 1) * LANES] for j in range(tk // LANES)]
        m_prev = m_ref[rows, :]
        m_blk = functools.reduce(jnp.maximum, blocks)
        m_new = jnp.maximum(m_prev, jnp.max(m_blk, axis=-1, keepdims=True))
        alpha = jnp.exp2(m_prev - m_new)
        p = jnp.concatenate([jnp.exp2(blk - m_new) for blk in blocks], axis=1).astype(BF16)
        acc_ref[rows, :] = jnp.concatenate([alpha, alpha], axis=1) * acc_ref[rows, :] + _dot(p, v1)
        m_ref[rows, :] = m_new

    unroll = math.gcd(n_chain, 4)

    def body(j, carry):
        for u in range(unroll):
            start = pl.multiple_of((j * unroll + u) * tk, tk)
            for c in range(n_chain):
                chain_step(c, start, False)
        return carry

    lax.fori_loop(0, i * n_chain // unroll, body, 0)
    for c in range(n_chain):
        for jj in range(c + 1):
            chain_step(c, pl.multiple_of(i * tq + jj * tk, tk), jj == c)
    acc = acc_ref[...]
    o_ref[0] = (acc[:, 0:HEAD_DIM] / acc[:, HEAD_DIM:2 * HEAD_DIM]).astype(o_ref.dtype)


def _attention(q, k, v, *, tq, tk):
    b, h, s, _ = q.shape
    tq, tk = min(tq, s), min(tk, s)
    return pl.pallas_call(
        functools.partial(_attn_kernel, tq=tq, tk=tk),
        grid=(b, h, s // tq),
        in_specs=[pl.BlockSpec((1, 1, tq, MLA_QK), lambda bi, hi, i: (bi, hi, i, 0)),
                  pl.BlockSpec((1, 1, s, MLA_QK), lambda bi, hi, i: (bi, hi, 0, 0)),
                  pl.BlockSpec((1, s, HEAD_DIM), lambda bi, hi, i: (bi, 0, hi))],
        out_specs=pl.BlockSpec((1, tq, HEAD_DIM), lambda bi, hi, i: (bi, i, hi)),
        out_shape=jax.ShapeDtypeStruct((b, s, MIX_W), BF16),
        scratch_shapes=[pltpu.VMEM((tq, LANES), F32), pltpu.VMEM((tq, 2 * HEAD_DIM), F32)],
        compiler_params=_params(("arbitrary", "arbitrary", "arbitrary")),
        name="mla_attention",
    )(q, k, v)


def _lane_scan(x, combine, identity):
    lane = lax.broadcasted_iota(jnp.int32, x.shape, 1)
    shift = 1
    while shift < LANES:
        x = combine(x, jnp.where(lane >= shift, pltpu.roll(x, shift, axis=1), identity))
        shift *= 2
    return x


def _gate_prep_kernel(ig_ref, fg_ref, r_ref, eloc_ref, a_ref, b_ref, cols_ref, mprev_ref, stack_ref):
    nc = ig_ref.shape[1]
    rows = nc * SUBLANES
    L = CHUNK
    ig = ig_ref[0].reshape(rows, L)
    fg = fg_ref[0].reshape(rows, L)
    lf = jnp.minimum(fg, 0.0) - jnp.log(1.0 + jnp.exp(-jnp.abs(fg)))
    g = _lane_scan(lf, jnp.add, 0.0)
    g_last = jnp.broadcast_to(g[:, L - 1:L], (rows, L))
    r = ig - g
    w_loc = g_last + r
    m_loc = jnp.broadcast_to(jnp.max(w_loc, axis=-1, keepdims=True), (rows, L))
    eloc_ref[0] = jnp.exp(w_loc - m_loc).reshape(nc, SUBLANES, L)
    r_ref[0] = r.reshape(nc, SUBLANES, L)
    cmax = _lane_scan(r, jnp.maximum, NEG_BIG)

    m = jnp.zeros((SUBLANES, L), F32)
    for c in range(nc):
        sl = slice(c * SUBLANES, (c + 1) * SUBLANES)
        mprev_ref[sl, :] = m
        m = jnp.maximum(g_last[sl, :] + m, m_loc[sl, :])
    m_prev = mprev_ref[...]
    mx = jnp.maximum(m_prev, cmax)
    m_new = jnp.maximum(g_last + m_prev, m_loc)
    a_ref[0] = jnp.exp(g_last + m_prev - m_new).reshape(nc, SUBLANES, L)
    b_ref[0] = jnp.exp(m_loc - m_new).reshape(nc, SUBLANES, L)

    stack_ref[...] = jnp.zeros(stack_ref.shape, F32)
    stack_ref[:, 0:SUBLANES, :] = mx.reshape(nc, SUBLANES, L)
    stack_ref[:, SUBLANES:2 * SUBLANES, :] = jnp.exp(m_prev - mx).reshape(nc, SUBLANES, L)
    stack_ref[:, 2 * SUBLANES:3 * SUBLANES, :] = jnp.exp(-g - mx).reshape(nc, SUBLANES, L)

    def body(c, carry):
        cols_ref[0, c] = stack_ref[c].T
        return carry

    lax.fori_loop(0, nc, body, 0)


def _gate_prep(ig, fg):
    b, nc, _, _ = ig.shape
    rows = pl.BlockSpec((1, nc, SUBLANES, CHUNK), lambda i: (i, 0, 0, 0))
    cols = pl.BlockSpec((1, nc, CHUNK, LANES), lambda i: (i, 0, 0, 0))
    row_shape = jax.ShapeDtypeStruct((b, nc, SUBLANES, CHUNK), F32)
    return pl.pallas_call(
        _gate_prep_kernel,
        grid=(b,),
        in_specs=[rows, rows],
        out_specs=[rows, rows, rows, rows, cols],
        out_shape=[row_shape, row_shape, row_shape, row_shape,
                   jax.ShapeDtypeStruct((b, nc, CHUNK, LANES), F32)],
        scratch_shapes=[pltpu.VMEM((nc * SUBLANES, CHUNK), F32), pltpu.VMEM((nc, CHUNK, LANES), F32)],
        compiler_params=_params(("arbitrary",)),
        name="mlstm_gate_prep",
    )(ig, fg)


def _head_norm_gate(hh, g_row, gate):
    mu = jnp.mean(hh, axis=-1, keepdims=True)
    hc = hh - mu
    var = jnp.mean(hc * hc, axis=-1, keepdims=True)
    return hc * lax.rsqrt(var + NORM_EPS) * g_row * gate


def _recurrent_kernel(mq_ref, mk_ref, mkt_ref, mv_ref, r_ref, eloc_ref, a_ref, b_ref, cols_ref, mo_ref,
                      rq_ref, rk_ref, rkt_ref, rv_ref, rg_ref, mg_ref, rgn_ref, yml_ref, yrt_ref,
                      c_ref, r_state_ref, dec_ref):
    L = CHUNK
    row = lax.broadcasted_iota(jnp.int32, (L, L), 0)
    col = lax.broadcasted_iota(jnp.int32, (L, L), 1)
    causal = col <= row

    @pl.when(pl.program_id(0) == 0)
    def _():
        c_ref[...] = jnp.zeros(c_ref.shape, F32)
        r_state_ref[...] = jnp.zeros(r_state_ref.shape, F32)
        rowf = row.astype(F32)
        colf = col.astype(F32)
        for h in range(N_HEADS):
            lg = math.log(1.0 - 2.0 ** (-5.0 - h))
            dec_ref[h, 0] = jnp.where(causal, jnp.exp(jnp.where(causal, rowf - colf, 0.0) * lg), 0.0)
            dec_ref[h, 1] = jnp.exp((rowf + 1.0) * lg)
            dec_ref[h, 2] = jnp.exp((L - 1.0 - colf) * lg)

    ones = jnp.ones((L, HEAD_DIM), BF16)
    chains = [(bi, h) for bi in range(mq_ref.shape[0]) for h in range(N_HEADS)]
    head = lambda h: slice(h * HEAD_DIM, (h + 1) * HEAD_DIM)
    cols = [cols_ref[bi, 0] for bi in range(mq_ref.shape[0])]

    ml_scores = [_dot_nt(mq_ref[bi, :, head(h)], mk_ref[bi, :, head(h)]) for bi, h in chains]
    rt_scores = [_dot_nt(rq_ref[bi, :, head(h)], rk_ref[bi, :, head(h)]) for bi, h in chains]
    ml_inter = [_dot(mq_ref[bi, :, head(h)], c_ref[bi * N_HEADS + h].astype(BF16)) for bi, h in chains]
    rt_inter = [_dot(rq_ref[bi, :, head(h)], r_state_ref[bi * N_HEADS + h].astype(BF16)) for bi, h in chains]
    v1s = [jnp.concatenate([mv_ref[bi, :, head(h)], ones], axis=1) for bi, h in chains]

    ml_qk = []
    for (bi, h), s in zip(chains, ml_scores):
        p = jnp.exp(jnp.where(causal, r_ref[bi, 0][h:h + 1, :] - cols[bi][:, h:h + 1], NEG_BIG))
        ml_qk.append((s * p).astype(BF16))
    rt_qk = [(s * dec_ref[h, 0]).astype(BF16) for (bi, h), s in zip(chains, rt_scores)]

    ml_intra = [_dot(qk, v1) for qk, v1 in zip(ml_qk, v1s)]
    rt_intra = [_dot(qk, rv_ref[bi, :, head(h)]) for (bi, h), qk in zip(chains, rt_qk)]
    ml_kwt = [(mkt_ref[bi, head(h), :].astype(F32) * eloc_ref[bi, 0][h:h + 1, :]).astype(BF16)
              for bi, h in chains]
    rt_kwt = [(rkt_ref[bi, head(h), :].astype(F32) * dec_ref[h, 2]).astype(BF16) for bi, h in chains]
    ml_upd = [_dot(kwt, v1) for kwt, v1 in zip(ml_kwt, v1s)]
    rt_upd = [_dot(kwt, rv_ref[bi, :, head(h)]) for (bi, h), kwt in zip(chains, rt_kwt)]

    for i, (bi, h) in enumerate(chains):
        st = bi * N_HEADS + h
        e_inter_c = cols[bi][:, SUBLANES + h:SUBLANES + h + 1]
        e_floor_c = cols[bi][:, 2 * SUBLANES + h:2 * SUBLANES + h + 1]
        nd = ml_intra[i] + e_inter_c * ml_inter[i]
        hh = nd[:, 0:HEAD_DIM] / jnp.maximum(jnp.abs(nd[:, HEAD_DIM:2 * HEAD_DIM]), e_floor_c)
        yml_ref[bi, :, head(h)] = _head_norm_gate(
            hh, mg_ref[:, head(h)], mo_ref[bi, :, head(h)].astype(F32)).astype(BF16)
        yrt_ref[bi, :, head(h)] = _head_norm_gate(
            rt_intra[i] + dec_ref[h, 1] * rt_inter[i], rgn_ref[:, head(h)],
            rg_ref[bi, :, head(h)].astype(F32)).astype(BF16)
        c_ref[st] = a_ref[bi, 0][h:h + 1, 0:1] * c_ref[st] + b_ref[bi, 0][h:h + 1, 0:1] * ml_upd[i]
        g_chunk = math.exp(L * math.log(1.0 - 2.0 ** (-5.0 - h)))
        r_state_ref[st] = g_chunk * r_state_ref[st] + rt_upd[i]


def _recurrent(mq, mk, mkt, mv, r, eloc, a, bsc, cols, mo, rq, rk, rkt, rv, rg, mlstm_norm_g, ret_norm_g):
    b, s, _ = mq.shape
    tok = pl.BlockSpec((b, CHUNK, MIX_W), lambda c: (0, c, 0))
    tok_t = pl.BlockSpec((b, MIX_W, CHUNK), lambda c: (0, 0, c))
    rows = pl.BlockSpec((b, 1, SUBLANES, CHUNK), lambda c: (0, c, 0, 0))
    colsp = pl.BlockSpec((b, 1, CHUNK, LANES), lambda c: (0, c, 0, 0))
    out = jax.ShapeDtypeStruct((b, s, MIX_W), BF16)
    return pl.pallas_call(
        _recurrent_kernel,
        grid=(s // CHUNK,),
        in_specs=[tok, tok, tok_t, tok, rows, rows, rows, rows, colsp, tok,
                  tok, tok, tok_t, tok, tok, _resident((1, MIX_W)), _resident((1, MIX_W))],
        out_specs=[tok, tok],
        out_shape=[out, out],
        scratch_shapes=[pltpu.VMEM((b * N_HEADS, HEAD_DIM, 2 * HEAD_DIM), F32),
                        pltpu.VMEM((b * N_HEADS, HEAD_DIM, HEAD_DIM), F32),
                        pltpu.VMEM((N_HEADS, 3, CHUNK, CHUNK), F32)],
        compiler_params=_params(("arbitrary",)),
        name="recurrent_mixers",
    )(mq, mk, mkt, mv, r, eloc, a, bsc, cols, mo, rq, rk, rkt, rv, rg, mlstm_norm_g, ret_norm_g)


def _merge_kernel(x_ref, ya_ref, yb_ref, yc_ref, gate_ref, wa_ref, wb_ref, wc_ref, wo_ref, g_ref, b_ref,
                  o_ref, *, alpha, sub):
    for r0 in range(0, x_ref.shape[0], sub):
        rows = slice(r0, r0 + sub)
        merged = None
        for br, (y_ref, w_ref) in enumerate(((ya_ref, wa_ref), (yb_ref, wb_ref), (yc_ref, wc_ref))):
            gate = gate_ref[rows, br * D_MODEL:(br + 1) * D_MODEL].astype(F32)
            term = gate * _dot(y_ref[rows, :], w_ref[...])
            merged = term if merged is None else merged + term
        mix = _dot(merged.astype(BF16), wo_ref[...])
        o_ref[rows, :] = _layer_norm(alpha * x_ref[rows, :] + mix, g_ref[...], b_ref[...])


def _merge(x, y_mla, y_ml, y_rt, gate, w_a, w_b, w_c, w_out, g, b, *, alpha, tm, sub):
    n = x.shape[0]
    tm = min(tm, n)
    sub = min(sub, tm)
    tok = lambda width: pl.BlockSpec((tm, width), lambda i: (i, 0))
    return pl.pallas_call(
        functools.partial(_merge_kernel, alpha=alpha, sub=sub),
        grid=(n // tm,),
        in_specs=[tok(D_MODEL), tok(MIX_W), tok(MIX_W), tok(MIX_W), tok(N_BRANCH * D_MODEL),
                  _resident((MIX_W, D_MODEL)), _resident((MIX_W, D_MODEL)), _resident((MIX_W, D_MODEL)),
                  _resident((D_MODEL, D_MODEL)), _resident((1, D_MODEL)), _resident((1, D_MODEL))],
        out_specs=tok(D_MODEL),
        out_shape=jax.ShapeDtypeStruct((n, D_MODEL), F32),
        compiler_params=_params(("arbitrary",)),
        name="merge_out_ln",
    )(x, y_mla, y_ml, y_rt, gate, w_a, w_b, w_c, w_out, g, b)


def _pack_w_in(w_in):
    o = 0
    parts = {}
    for name, width in (("cq", MLA_Q_LORA), ("ckv", MLA_KV_LORA), ("kr", MLA_ROPE), ("mq", MIX_W),
                        ("mk", MIX_W), ("mv", MIX_W), ("mi", N_HEADS), ("mf", N_HEADS), ("mo", MIX_W),
                        ("rq", MIX_W), ("rk", MIX_W), ("rv", MIX_W), ("rg", MIX_W),
                        ("gl", N_BRANCH * D_MODEL)):
        parts[name] = w_in[:, o:o + width]
        o += width
    hr = MLA_ROPE // 2
    kr1, kr2 = parts["kr"][:, :hr], parts["kr"][:, hr:]
    main = jnp.concatenate(
        [parts["cq"], parts["ckv"], kr1, kr1, kr2, kr2, parts["mq"], parts["mk"], parts["mv"],
         parts["mo"], parts["rq"], parts["rk"], parts["rv"], parts["rg"], parts["gl"]], axis=1)
    gates_t = jnp.concatenate([parts["mi"], parts["mi"], parts["mf"], parts["mf"]], axis=1).T
    return main.astype(BF16), gates_t.astype(BF16)


def _pack_w_uq(w_uq):
    per = MLA_NOPE + MLA_ROPE
    hr = MLA_ROPE // 2
    nope = [w_uq[:, h * per:h * per + MLA_NOPE] for h in range(N_HEADS)]
    x1 = [w_uq[:, h * per + MLA_NOPE:h * per + MLA_NOPE + hr] for h in range(N_HEADS)]
    x2 = [w_uq[:, h * per + MLA_NOPE + hr:(h + 1) * per] for h in range(N_HEADS)]
    pairs = [x1[0], x1[1], x2[0], x2[1], x1[2], x1[3], x2[2], x2[3]]
    return jnp.concatenate(nope + pairs, axis=1).astype(BF16)


def _pack_w_ukv(w_ukv):
    per = MLA_NOPE + HEAD_DIM
    keys = [w_ukv[:, h * per:h * per + MLA_NOPE] for h in range(N_HEADS)]
    vals = [w_ukv[:, h * per + MLA_NOPE:(h + 1) * per] for h in range(N_HEADS)]
    return jnp.concatenate(keys + vals, axis=1).astype(BF16)


def kernel(x, positions, ffn1_w_gate, ffn1_w_up, ffn1_w_down, ln1_g, ln1_b, w_in, mla_q_norm_g, mla_w_uq, mla_kv_norm_g, mla_w_ukv, mlstm_conv_w, mlstm_conv_b, mlstm_gate_b, mlstm_norm_g, ret_norm_g, w_br_mla, w_br_mlstm, w_br_ret, gate_b, w_out, ln2_g, ln2_b, ffn2_w_gate, ffn2_w_up, ffn2_w_down, ln3_g, ln3_b):
    b, s, d = x.shape
    depth = w_in.shape[0]
    alpha = (2 * depth) ** DEPTH_ALPHA_POW
    n = b * s
    tables = _rope_tables(positions, tm=ROPE_TILE)
    row = lambda a: a.reshape(1, -1)
    ff_split = (D_FF // MXU_WIDTH + 1) // 2 * MXU_WIDTH
    ffn = functools.partial(_ffn_ln, alpha=alpha, tm=TOKEN_TILE, sub=ROW_BLOCK,
                            ff_chunks=((0, ff_split), (ff_split, D_FF)))

    h = x.reshape(n, d)
    for l in range(depth):
        h = ffn(h, ffn1_w_gate, ffn1_w_up, ffn1_w_down, row(ln1_g[l]), row(ln1_b[l]), layer=l)

        w_main, w_gates_t = _pack_w_in(w_in[l])
        gbias = mlstm_gate_b[l]
        gate_bias = jnp.concatenate([gbias[:N_HEADS], gbias[:N_HEADS], gbias[N_HEADS:], gbias[N_HEADS:]]
                                    ).reshape(2 * SUBLANES, 1)
        (q, k, v, mq, mk, mkt, mv, ig, fg, mo, rq, rk, rkt, rv, rg, gate) = _in_proj(
            h.reshape(b, s, d), w_main, w_gates_t, gate_bias, row(mla_q_norm_g[l]), _pack_w_uq(mla_w_uq[l]),
            row(mla_kv_norm_g[l]), _pack_w_ukv(mla_w_ukv[l]), mlstm_conv_w[l], row(mlstm_conv_b[l]),
            row(gate_b[l]), tables, tm=IN_PROJ_TILE)
        y_mla = _attention(q, k, v, tq=ATTN_Q_TILE, tk=ATTN_K_TILE)
        r, eloc, a_sc, b_sc, cols = _gate_prep(ig, fg)
        y_ml, y_rt = _recurrent(mq, mk, mkt, mv, r, eloc, a_sc, b_sc, cols, mo, rq, rk, rkt, rv, rg,
                                row(mlstm_norm_g[l]), row(ret_norm_g[l]))
        h = _merge(h, y_mla.reshape(n, MIX_W), y_ml.reshape(n, MIX_W), y_rt.reshape(n, MIX_W),
                   gate.reshape(n, N_BRANCH * D_MODEL), w_br_mla[l].astype(BF16), w_br_mlstm[l].astype(BF16),
                   w_br_ret[l].astype(BF16), w_out[l].astype(BF16), row(ln2_g[l]), row(ln2_b[l]),
                   alpha=alpha, tm=TOKEN_TILE, sub=ROW_BLOCK)

        h = ffn(h, ffn2_w_gate, ffn2_w_up, ffn2_w_down, row(ln3_g[l]), row(ln3_b[l]), layer=l)
    return h.reshape(b, s, d)
```
